```python
import math
import jax, jax.numpy as jnp
from jax import lax
import numpy as np

D_MODEL = 1024
BATCH = 8
SEQ = 2048
DEPTH = 1
DEC_BATCH = 128
DEC_SEQ = 1
PAST_LEN = 8192
PAGE_SIZE = 128

HEAD_DIM = 64
N_HEADS_MOBA = 8
N_HEADS_DIFF = 4
DIFF_V_DIM = 2 * HEAD_DIM
WIDTH_MOBA = N_HEADS_MOBA * HEAD_DIM
WIDTH_DIFF_QK = N_HEADS_DIFF * 2 * HEAD_DIM
WIDTH_DIFF_V = N_HEADS_DIFF * DIFF_V_DIM
MIX_WIDTH = WIDTH_MOBA + WIDTH_DIFF_V
PROJ_SPLITS = [WIDTH_MOBA, 2 * WIDTH_MOBA, 3 * WIDTH_MOBA,
               3 * WIDTH_MOBA + WIDTH_DIFF_QK, 3 * WIDTH_MOBA + 2 * WIDTH_DIFF_QK]
PROJ_WIDTH = 3 * WIDTH_MOBA + 2 * WIDTH_DIFF_QK + WIDTH_DIFF_V
ROPE_THETA = 500000.0
ROT_DIM = HEAD_DIM // 4
MOBA_BLOCK = 256
MOBA_TOPK = 3
Q_BLOCK = 128
N_EXPERTS = 32
TOP_K = 4
D_FF = D_MODEL
SWIGLU_LIMIT = 7.0
SWIGLU_ALPHA = 1.702
MOE_ROWS = 128
RMS_EPS = 1e-5
NEG_INF = -1e30

kernel_name = "hymba_moba_diffattn_moe_step"

F32 = jnp.float32


def rmsnorm(x, g):
    xf = x.astype(F32)
    y = xf * lax.rsqrt(jnp.mean(xf * xf, axis=-1, keepdims=True) + RMS_EPS)
    return (y * g.astype(F32)).astype(x.dtype)


def rope(x, pos):
    half = ROT_DIM // 2
    inv = ROPE_THETA ** (-jnp.arange(half, dtype=F32) * (2.0 / ROT_DIM))
    ang = pos.astype(F32)[:, None] * inv[None, :]
    cos = jnp.cos(ang)[:, None, :]
    sin = jnp.sin(ang)[:, None, :]
    xf = x.astype(F32)
    x1, x2 = xf[..., :half], xf[..., half:ROT_DIM]
    out = jnp.concatenate([x1 * cos - x2 * sin, x2 * cos + x1 * sin, xf[..., ROT_DIM:]], axis=-1)
    return out.astype(x.dtype)


def mixer_inputs(x, g, w, pos):
    b, s, _ = x.shape
    p = jnp.einsum('bsd,de->bse', rmsnorm(x, g), w)
    qa, ka, va, qd, kd, vd = jnp.split(p, PROJ_SPLITS, axis=-1)
    qa = rope(qa.reshape(b, s, N_HEADS_MOBA, HEAD_DIM), pos)
    ka = rope(ka.reshape(b, s, N_HEADS_MOBA, HEAD_DIM), pos)
    va = va.reshape(b, s, N_HEADS_MOBA, HEAD_DIM)
    qd = rope(qd.reshape(b, s, 2 * N_HEADS_DIFF, HEAD_DIM), pos).reshape(b, s, N_HEADS_DIFF, 2 * HEAD_DIM)
    kd = rope(kd.reshape(b, s, 2 * N_HEADS_DIFF, HEAD_DIM), pos).reshape(b, s, N_HEADS_DIFF, 2 * HEAD_DIM)
    vd = vd.reshape(b, s, N_HEADS_DIFF, DIFF_V_DIM)
    return qa, ka, va, qd, kd, vd


def moba_prompt(q, k, v):
    b, s, h, dh = q.shape
    nb = -(-s // MOBA_BLOCK)
    pad = nb * MOBA_BLOCK - s
    qt = q.transpose(0, 2, 1, 3)
    padw = ((0, 0), (0, 0), (0, pad), (0, 0))
    kb = jnp.pad(k.transpose(0, 2, 1, 3), padw).reshape(b, h, nb, MOBA_BLOCK, dh)
    vb = jnp.pad(v.transpose(0, 2, 1, 3), padw).reshape(b, h, nb, MOBA_BLOCK, dh)
    kmean = jnp.mean(kb.astype(F32), axis=3)
    n_sel = min(MOBA_TOPK, nb - 1)
    scale = HEAD_DIM ** -0.5
    bi = jnp.arange(b)[:, None, None, None]
    hi = jnp.arange(h)[None, :, None, None]
    blk = jnp.arange(nb)

    def block(i):
        q0 = i * Q_BLOCK
        qb = lax.dynamic_slice_in_dim(qt, q0, Q_BLOCK, axis=2)
        qpos = q0 + jnp.arange(Q_BLOCK)
        own = q0 // MOBA_BLOCK
        k_own = lax.dynamic_index_in_dim(kb, own, axis=2, keepdims=False)
        v_own = lax.dynamic_index_in_dim(vb, own, axis=2, keepdims=False)
        kpos = own * MOBA_BLOCK + jnp.arange(MOBA_BLOCK)
        s_own = jnp.einsum('bhqd,bhkd->bhqk', qb, k_own, preferred_element_type=F32) * scale
        s_own = jnp.where(kpos[None, :] <= qpos[:, None], s_own, NEG_INF)
        if n_sel == 0:
            p = jax.nn.softmax(s_own, axis=-1).astype(v.dtype)
            o = jnp.einsum('bhqk,bhkd->bhqd', p, v_own, preferred_element_type=F32)
            return o.astype(q.dtype)
        gate = jnp.einsum('bhqd,bhnd->bhqn', qb.astype(F32), kmean)
        gate = jnp.where(blk < own, gate, NEG_INF)
        _, sel = lax.top_k(gate, n_sel)
        ks = kb[bi, hi, sel]
        vs = vb[bi, hi, sel]
        s_sel = jnp.einsum('bhqd,bhqnkd->bhqnk', qb, ks, preferred_element_type=F32) * scale
        s_sel = jnp.where((sel < own)[..., None], s_sel, NEG_INF)
        n_k = n_sel * MOBA_BLOCK
        scores = jnp.concatenate([s_sel.reshape(b, h, Q_BLOCK, n_k), s_own], axis=-1)
        p = jax.nn.softmax(scores, axis=-1).astype(v.dtype)
        p_sel = p[..., :n_k].reshape(b, h, Q_BLOCK, n_sel, MOBA_BLOCK)
        o = (jnp.einsum('bhqnk,bhqnkd->bhqd', p_sel, vs, preferred_element_type=F32)
             + jnp.einsum('bhqk,bhkd->bhqd', p[..., n_k:], v_own, preferred_element_type=F32))
        return o.astype(q.dtype)

    o = lax.map(block, jnp.arange(s // Q_BLOCK))
    return o.transpose(1, 0, 3, 2, 4).reshape(b, s, h * dh)


def moba_sample(q, k, v, ck, cv, page_table, layer):
    b, sq, h, dh = q.shape
    n_pages = page_table.shape[1]
    ppb = MOBA_BLOCK // PAGE_SIZE
    nb = (n_pages * PAGE_SIZE) // MOBA_BLOCK
    n_own_pages = n_pages - nb * ppb
    n_past_own = n_own_pages * PAGE_SIZE
    scale = HEAD_DIM ** -0.5
    qt, kt, vt = (a.transpose(0, 2, 1, 3) for a in (q, k, v))
    own_pages = page_table[:, nb * ppb:]

    def own_rows(c, new):
        past = c[own_pages, layer].transpose(0, 2, 1, 3, 4).reshape(b, h, n_past_own, dh)
        return jnp.concatenate([past, new], axis=2)

    k_own = own_rows(ck, kt)
    v_own = own_rows(cv, vt)
    visible = jnp.arange(n_past_own + sq)[None, :] <= n_past_own + jnp.arange(sq)[:, None]
    s_own = jnp.einsum('bhqd,bhkd->bhqk', qt, k_own, preferred_element_type=F32) * scale
    s_own = jnp.where(visible, s_own, NEG_INF)
    n_sel = min(MOBA_TOPK, nb)
    if n_sel == 0:
        p = jax.nn.softmax(s_own, axis=-1).astype(v.dtype)
        o = jnp.einsum('bhqk,bhkd->bhqd', p, v_own, preferred_element_type=F32)
        return o.astype(q.dtype).transpose(0, 2, 1, 3).reshape(b, sq, h * dh)

    def page_mean(pg):
        return jnp.mean(ck[page_table[:, pg], layer].astype(F32), axis=2)

    pm = lax.map(page_mean, jnp.arange(nb * ppb))
    kmean = pm.reshape(nb, ppb, b, h, dh).mean(axis=1)
    gate = jnp.einsum('bhqd,nbhd->bhqn', qt.astype(F32), kmean)
    _, sel = lax.top_k(gate, n_sel)
    logical = sel[..., None] * ppb + jnp.arange(ppb)
    phys = page_table[jnp.arange(b)[:, None, None, None, None], logical]
    hi = jnp.arange(h)[None, :, None, None, None]
    ks = ck[phys, layer, hi].reshape(b, h, sq, n_sel, MOBA_BLOCK, dh)
    vs = cv[phys, layer, hi].reshape(b, h, sq, n_sel, MOBA_BLOCK, dh)
    s_sel = jnp.einsum('bhqd,bhqnkd->bhqnk', qt, ks, preferred_element_type=F32) * scale
    n_k = n_sel * MOBA_BLOCK
    scores = jnp.concatenate([s_sel.reshape(b, h, sq, n_k), s_own], axis=-1)
    p = jax.nn.softmax(scores, axis=-1).astype(v.dtype)
    p_sel = p[..., :n_k].reshape(b, h, sq, n_sel, MOBA_BLOCK)
    o = (jnp.einsum('bhqnk,bhqnkd->bhqd', p_sel, vs, preferred_element_type=F32)
         + jnp.einsum('bhqk,bhkd->bhqd', p[..., n_k:], v_own, preferred_element_type=F32))
    return o.astype(q.dtype).transpose(0, 2, 1, 3).reshape(b, sq, h * dh)


def diff_prompt(qd, kd, vd, lam, g_sub, out_scale):
    b, s, hd, _ = qd.shape
    t = lambda a: a.transpose(0, 2, 1, 3)
    q1, q2 = t(qd[..., :HEAD_DIM]), t(qd[..., HEAD_DIM:])
    k1, k2 = t(kd[..., :HEAD_DIM]), t(kd[..., HEAD_DIM:])
    vf = t(vd).astype(F32)
    scale = HEAD_DIM ** -0.5
    kpos = jnp.arange(s)

    def block(i):
        q0 = i * Q_BLOCK
        causal = kpos[None, :] <= (q0 + jnp.arange(Q_BLOCK))[:, None]

        def attn_map(qf, kf):
            qb = lax.dynamic_slice_in_dim(qf, q0, Q_BLOCK, axis=2)
            sc = jnp.einsum('bhqd,bhkd->bhqk', qb, kf, preferred_element_type=F32) * scale
            return jax.nn.softmax(jnp.where(causal, sc, NEG_INF), axis=-1)

        a = attn_map(q1, k1) - lam * attn_map(q2, k2)
        return jnp.einsum('bhqk,bhkd->bhqd', a, vf)

    o = lax.map(block, jnp.arange(s // Q_BLOCK))
    o = o.transpose(1, 0, 3, 2, 4).reshape(b, s, hd, DIFF_V_DIM)
    o = rmsnorm(o, g_sub) * out_scale
    return o.reshape(b, s, hd * DIFF_V_DIM)


def diff_sample(qd, kd, vd, ck, cv, page_table, layer, lam, g_sub, out_scale):
    b, sq, hd, _ = qd.shape
    t = lambda a: a.transpose(0, 2, 1, 3)
    q1, q2 = t(qd[..., :HEAD_DIM]), t(qd[..., HEAD_DIM:])
    k1, k2 = t(kd[..., :HEAD_DIM]), t(kd[..., HEAD_DIM:])
    vf = t(vd).astype(F32)
    scale = HEAD_DIM ** -0.5
    tri = jnp.arange(sq)[None, :] <= jnp.arange(sq)[:, None]

    def sc(qf, kf):
        return jnp.einsum('bhqd,bhkd->bhqk', qf, kf, preferred_element_type=F32) * scale

    def start(s):
        s = jnp.where(tri, s, NEG_INF)
        m = jnp.max(s, axis=-1)
        e = jnp.exp(s - m[..., None])
        return (m, jnp.sum(e, axis=-1), jnp.einsum('bhqk,bhkd->bhqd', e, vf))

    def update(st, s, vp):
        m, l, acc = st
        m_new = jnp.maximum(m, jnp.max(s, axis=-1))
        c = jnp.exp(m - m_new)
        e = jnp.exp(s - m_new[..., None])
        return (m_new, l * c + jnp.sum(e, axis=-1), acc * c[..., None] + jnp.einsum('bhqk,bhkd->bhqd', e, vp))

    def step(carry, pg):
        st1, st2 = carry
        phys = page_table[:, pg]
        kp = ck[phys, layer]
        vp = cv[phys, layer].astype(F32)
        st1 = update(st1, sc(q1, kp[..., :HEAD_DIM]), vp)
        st2 = update(st2, sc(q2, kp[..., HEAD_DIM:]), vp)
        return (st1, st2), None

    (st1, st2), _ = lax.scan(step, (start(sc(q1, k1)), start(sc(q2, k2))), jnp.arange(page_table.shape[1]))
    o = st1[2] / st1[1][..., None] - lam * (st2[2] / st2[1][..., None])
    o = rmsnorm(o.transpose(0, 2, 1, 3), g_sub) * out_scale
    return o.reshape(b, sq, hd * DIFF_V_DIM)


def moe(x, w_router, b_router, w_gate, b_gate, w_up, b_up, w_down, b_down):
    d = x.shape[-1]
    x2 = x.reshape(-1, d)
    t = x2.shape[0]
    logits = jnp.einsum('td,de->te', x2, w_router, preferred_element_type=F32) + b_router.astype(F32)
    top_val, top_idx = lax.top_k(logits, TOP_K)
    gates = jax.nn.softmax(top_val, axis=-1)
    a = t * TOP_K
    flat_e = top_idx.reshape(a)
    order = jnp.argsort(flat_e)
    e_sorted = flat_e[order]
    tok_sorted = order // TOP_K
    gate_sorted = gates.reshape(a)[order]
    sizes = jnp.bincount(flat_e, length=N_EXPERTS)
    starts = jnp.cumsum(sizes) - sizes
    padded = (sizes + MOE_ROWS - 1) // MOE_ROWS * MOE_ROWS
    pends = jnp.cumsum(padded)
    dest = (pends - padded)[e_sorted] + jnp.arange(a) - starts[e_sorted]
    n_blocks = -(-(a + N_EXPERTS * (MOE_ROWS - 1)) // MOE_ROWS)
    buf = jnp.zeros((n_blocks * MOE_ROWS, d), x2.dtype).at[dest].set(x2[tok_sorted])
    block_e = jnp.minimum(jnp.searchsorted(pends, jnp.arange(n_blocks) * MOE_ROWS, side='right'), N_EXPERTS - 1)

    def expert_block(args):
        xb, e = args
        g = jnp.einsum('rd,df->rf', xb, w_gate[e], preferred_element_type=F32) + b_gate[e]
        u = jnp.einsum('rd,df->rf', xb, w_up[e], preferred_element_type=F32) + b_up[e]
        g = jnp.minimum(g, SWIGLU_LIMIT)
        u = jnp.clip(u, -SWIGLU_LIMIT, SWIGLU_LIMIT)
        hdn = (g * jax.nn.sigmoid(SWIGLU_ALPHA * g) * (u + 1.0)).astype(xb.dtype)
        return jnp.einsum('rf,fd->rd', hdn, w_down[e], preferred_element_type=F32) + b_down[e]

    yb = lax.map(expert_block, (buf.reshape(n_blocks, MOE_ROWS, d), block_e)).reshape(n_blocks * MOE_ROWS, d)
    y = jax.ops.segment_sum(yb[dest] * gate_sorted[:, None], tok_sorted, num_segments=t)
    return y.astype(x.dtype).reshape(x.shape)


def setup_inputs(seed: int = 0) -> dict:
    key = jax.random.key(seed)
    k = jax.random.split(key, 26)
    nrm = jax.random.normal
    n_pages = PAST_LEN // PAGE_SIZE
    n_used = DEC_BATCH * n_pages
    n_phys = n_used + n_used // 4
    page_table = jax.random.permutation(k[0], n_phys)[:n_used].reshape(DEC_BATCH, n_pages).astype(jnp.int32)
    return {
        "x_prompt": nrm(k[1], (BATCH, SEQ, D_MODEL), F32),
        "x_sample": nrm(k[2], (DEC_BATCH, DEC_SEQ, D_MODEL), F32),
        "cache_k_moba": nrm(k[3], (n_phys, DEPTH, N_HEADS_MOBA, PAGE_SIZE, HEAD_DIM), F32),
        "cache_v_moba": nrm(k[4], (n_phys, DEPTH, N_HEADS_MOBA, PAGE_SIZE, HEAD_DIM), F32),
        "cache_k_diff": nrm(k[5], (n_phys, DEPTH, N_HEADS_DIFF, PAGE_SIZE, 2 * HEAD_DIM), F32),
        "cache_v_diff": nrm(k[6], (n_phys, DEPTH, N_HEADS_DIFF, PAGE_SIZE, DIFF_V_DIM), F32),
        "page_table": page_table,
        "g_mix": 1.0 + 0.02 * nrm(k[7], (DEPTH, D_MODEL), F32),
        "w_in": nrm(k[8], (DEPTH, D_MODEL, PROJ_WIDTH), F32) * D_MODEL ** -0.5,
        "lambda_q1": 0.1 * nrm(k[9], (DEPTH, HEAD_DIM), F32),
        "lambda_k1": 0.1 * nrm(k[10], (DEPTH, HEAD_DIM), F32),
        "lambda_q2": 0.1 * nrm(k[11], (DEPTH, HEAD_DIM), F32),
        "lambda_k2": 0.1 * nrm(k[12], (DEPTH, HEAD_DIM), F32),
        "g_subln": 1.0 + 0.02 * nrm(k[13], (DEPTH, DIFF_V_DIM), F32),
        "w_out": nrm(k[14], (DEPTH, MIX_WIDTH, D_MODEL), F32) * MIX_WIDTH ** -0.5,
        "g_ffn": 1.0 + 0.02 * nrm(k[15], (DEPTH, D_MODEL), F32),
        "w_router": nrm(k[16], (DEPTH, D_MODEL, N_EXPERTS), F32) * D_MODEL ** -0.5,
        "b_router": 0.01 * nrm(k[17], (DEPTH, N_EXPERTS), F32),
        "w_gate": nrm(k[18], (DEPTH, N_EXPERTS, D_MODEL, D_FF), F32) * D_MODEL ** -0.5,
        "b_gate": 0.01 * nrm(k[19], (DEPTH, N_EXPERTS, D_FF), F32),
        "w_up": nrm(k[20], (DEPTH, N_EXPERTS, D_MODEL, D_FF), F32) * D_MODEL ** -0.5,
        "b_up": 0.01 * nrm(k[21], (DEPTH, N_EXPERTS, D_FF), F32),
        "w_down": nrm(k[22], (DEPTH, N_EXPERTS, D_FF, D_MODEL), F32) * D_FF ** -0.5,
        "b_down": 0.01 * nrm(k[23], (DEPTH, N_EXPERTS, D_MODEL), F32),
        "g_final": 1.0 + 0.02 * nrm(k[24], (D_MODEL,), F32),
    }


def reference(x_prompt, x_sample, cache_k_moba, cache_v_moba, cache_k_diff, cache_v_diff, page_table,
              g_mix, w_in, lambda_q1, lambda_k1, lambda_q2, lambda_k2, g_subln, w_out, g_ffn,
              w_router, b_router, w_gate, b_gate, w_up, b_up, w_down, b_down, g_final):
    seq = x_prompt.shape[1]
    dec_seq = x_sample.shape[1]
    past_len = page_table.shape[1] * PAGE_SIZE
    pos_p = jnp.arange(seq, dtype=jnp.int32)
    pos_s = past_len + jnp.arange(dec_seq, dtype=jnp.int32)
    xp, xs = x_prompt, x_sample
    kmp, vmp, kdp, vdp, kms, vms, kds, vds = [], [], [], [], [], [], [], []
    for l in range(DEPTH):
        lam_init = 0.8 - 0.6 * math.exp(-0.3 * l)
        lam = (jnp.exp(jnp.sum(lambda_q1[l].astype(F32) * lambda_k1[l].astype(F32)))
               - jnp.exp(jnp.sum(lambda_q2[l].astype(F32) * lambda_k2[l].astype(F32))) + lam_init)
        out_scale = 1.0 - lam_init
        ffn = (w_router[l], b_router[l], w_gate[l], b_gate[l], w_up[l], b_up[l], w_down[l], b_down[l])

        qa, ka, va, qd, kd, vd = mixer_inputs(xp, g_mix[l], w_in[l], pos_p)
        mix = jnp.concatenate([moba_prompt(qa, ka, va).astype(xp.dtype),
                               diff_prompt(qd, kd, vd, lam, g_subln[l], out_scale).astype(xp.dtype)], axis=-1)
        xp = xp + jnp.einsum('bse,ed->bsd', mix, w_out[l])
        xp = xp + moe(rmsnorm(xp, g_ffn[l]), *ffn)
        kmp.append(ka.transpose(0, 2, 1, 3))
        vmp.append(va.transpose(0, 2, 1, 3))
        kdp.append(kd.transpose(0, 2, 1, 3))
        vdp.append(vd.transpose(0, 2, 1, 3))

        qa, ka, va, qd, kd, vd = mixer_inputs(xs, g_mix[l], w_in[l], pos_s)
        mix = jnp.concatenate(
            [moba_sample(qa, ka, va, cache_k_moba, cache_v_moba, page_table, l).astype(xs.dtype),
             diff_sample(qd, kd, vd, cache_k_diff, cache_v_diff, page_table, l, lam, g_subln[l], out_scale).astype(xs.dtype)],
            axis=-1)
        xs = xs + jnp.einsum('bse,ed->bsd', mix, w_out[l])
        xs = xs + moe(rmsnorm(xs, g_ffn[l]), *ffn)
        kms.append(ka.transpose(0, 2, 1, 3))
        vms.append(va.transpose(0, 2, 1, 3))
        kds.append(kd.transpose(0, 2, 1, 3))
        vds.append(vd.transpose(0, 2, 1, 3))

    y_prompt = rmsnorm(xp, g_final)
    y_sample = rmsnorm(xs, g_final)
    k_moba_prompt = jnp.stack(kmp, axis=1)
    v_moba_prompt = jnp.stack(vmp, axis=1)
    k_diff_prompt = jnp.stack(kdp, axis=1)
    v_diff_prompt = jnp.stack(vdp, axis=1)
    k_moba_sample = jnp.stack(kms, axis=1)
    v_moba_sample = jnp.stack(vms, axis=1)
    k_diff_sample = jnp.stack(kds, axis=1)
    v_diff_sample = jnp.stack(vds, axis=1)
    return (y_prompt, y_sample, k_moba_prompt, v_moba_prompt, k_diff_prompt, v_diff_prompt,
            k_moba_sample, v_moba_sample, k_diff_sample, v_diff_sample)
```

```python
import functools
import math

import jax
import jax.numpy as jnp
from jax import lax
from jax.experimental import pallas as pl
from jax.experimental.pallas import tpu as pltpu

F32 = jnp.float32
BF16 = jnp.bfloat16
I32 = jnp.int32

HEAD_DIM = 64
N_HEADS_MOBA = 8
N_HEADS_DIFF = 4
WIDTH = 512
ROPE_THETA = 500000.0
ROT_DIM = HEAD_DIM // 4
MOBA_BLOCK = 256
MOBA_TOPK = 3
PAGE_SIZE = 128
TOP_K = 4
SWIGLU_LIMIT = 7.0
SWIGLU_ALPHA = 1.702
RMS_EPS = 1e-5
NEG_INF = -1e30
SCALE = HEAD_DIM ** -0.5

LANES = 128
MOE_ROWS = 256
VMEM_LIMIT = 48 * 1024 * 1024

_NT = (((1,), (1,)), ((), ()))
_HI = lax.Precision.HIGHEST


def _cparams(*sem):
    return pltpu.CompilerParams(dimension_semantics=sem, vmem_limit_bytes=VMEM_LIMIT)


def _rope_tables(pos):
    half = ROT_DIM // 2
    inv = ROPE_THETA ** (-jnp.arange(half, dtype=F32) * (2.0 / ROT_DIM))
    ang = pos.astype(F32)[:, None] * inv[None, :]
    cos = jnp.cos(ang)
    sin = jnp.sin(ang)
    s = pos.shape[0]
    pad = HEAD_DIM - ROT_DIM
    c64 = jnp.concatenate([cos, cos, jnp.ones((s, pad), F32)], axis=1)
    a64 = jnp.concatenate([-sin, jnp.zeros((s, half + pad), F32)], axis=1)
    b64 = jnp.concatenate([jnp.zeros((s, half), F32), sin, jnp.zeros((s, pad), F32)], axis=1)
    rep = LANES // HEAD_DIM
    return jnp.tile(c64, (1, rep)), jnp.tile(a64, (1, rep)), jnp.tile(b64, (1, rep))


def _norm_rows(x, g):
    ms = jnp.mean(x * x, axis=-1, keepdims=True)
    return x * lax.rsqrt(ms + RMS_EPS) * g


def _rope_section(p, c, sa, sb):
    half = ROT_DIM // 2
    outs = []
    for j in range(p.shape[1] // LANES):
        xc = p[:, j * LANES:(j + 1) * LANES]
        outs.append(xc * c + pltpu.roll(xc, LANES - half, 1) * sa + pltpu.roll(xc, half, 1) * sb)
    return jnp.concatenate(outs, axis=1)


def _inproj_sections(x_ref, g_ref, w_ref, c_ref, sa_ref, sb_ref):
    xn = _norm_rows(x_ref[0], g_ref[...]).astype(BF16)
    c, sa, sb = c_ref[...], sa_ref[...], sb_ref[...]
    secs = []
    for s in range(6):
        p = jnp.dot(xn, w_ref[:, s * WIDTH:(s + 1) * WIDTH], preferred_element_type=F32)
        if s in (0, 1, 3, 4):
            p = _rope_section(p, c, sa, sb)
        secs.append(p)
    return secs


def _inproj_prompt_kernel(x_ref, g_ref, w_ref, c_ref, sa_ref, sb_ref,
                          qa_ref, ka_ref, va_ref, qd_ref, kahm_ref, vahm_ref, kdhm_ref, vdhm_ref):
    qa, ka, va, qd, kd, vd = _inproj_sections(x_ref, g_ref, w_ref, c_ref, sa_ref, sb_ref)
    qa_ref[0] = qa
    ka_ref[0] = ka
    va_ref[0] = va
    qd_ref[0] = qd
    for h in range(N_HEADS_MOBA):
        kahm_ref[0, 0, h] = ka[:, h * HEAD_DIM:(h + 1) * HEAD_DIM]
        vahm_ref[0, 0, h] = va[:, h * HEAD_DIM:(h + 1) * HEAD_DIM]
    for h in range(N_HEADS_DIFF):
        kdhm_ref[0, 0, h] = kd[:, h * LANES:(h + 1) * LANES]
        vdhm_ref[0, 0, h] = vd[:, h * LANES:(h + 1) * LANES]


def _inproj_sample_kernel(x_ref, g_ref, w_ref, c_ref, sa_ref, sb_ref, p_ref):
    secs = _inproj_sections(x_ref, g_ref, w_ref, c_ref, sa_ref, sb_ref)
    for s in range(6):
        p_ref[0, :, s * WIDTH:(s + 1) * WIDTH] = secs[s]


def _inproj_in_specs(tm, d, pw):
    return [
        pl.BlockSpec((1, tm, d), lambda b, i: (b, i, 0)),
        pl.BlockSpec((1, d), lambda b, i: (0, 0)),
        pl.BlockSpec((d, pw), lambda b, i: (0, 0)),
        pl.BlockSpec((tm, LANES), lambda b, i: (i, 0)),
        pl.BlockSpec((tm, LANES), lambda b, i: (i, 0)),
        pl.BlockSpec((tm, LANES), lambda b, i: (i, 0)),
    ]


def _inproj_prompt(x, g, w_bf, tabs, tm):
    b, s, d = x.shape
    pw = w_bf.shape[1]
    tok = lambda: pl.BlockSpec((1, tm, WIDTH), lambda bb, i: (bb, i, 0))
    hm_a = lambda: pl.BlockSpec((1, 1, N_HEADS_MOBA, tm, HEAD_DIM), lambda bb, i: (bb, 0, 0, i, 0))
    hm_d = lambda: pl.BlockSpec((1, 1, N_HEADS_DIFF, tm, LANES), lambda bb, i: (bb, 0, 0, i, 0))
    tok_shape = jax.ShapeDtypeStruct((b, s, WIDTH), F32)
    hma_shape = jax.ShapeDtypeStruct((b, 1, N_HEADS_MOBA, s, HEAD_DIM), F32)
    hmd_shape = jax.ShapeDtypeStruct((b, 1, N_HEADS_DIFF, s, LANES), F32)
    return pl.pallas_call(
        _inproj_prompt_kernel,
        grid=(b, s // tm),
        in_specs=_inproj_in_specs(tm, d, pw),
        out_specs=[tok(), tok(), tok(), tok(), hm_a(), hm_a(), hm_d(), hm_d()],
        out_shape=[tok_shape, tok_shape, tok_shape, tok_shape, hma_shape, hma_shape, hmd_shape, hmd_shape],
        compiler_params=_cparams("parallel", "parallel"),
    )(x, g, w_bf, *tabs)


def _inproj_sample(x, g, w_bf, tabs):
    b, s, d = x.shape
    pw = w_bf.shape[1]
    return pl.pallas_call(
        _inproj_sample_kernel,
        grid=(b, 1),
        in_specs=_inproj_in_specs(s, d, pw),
        out_specs=pl.BlockSpec((1, s, pw), lambda bb, i: (bb, 0, 0)),
        out_shape=jax.ShapeDtypeStruct((b, s, pw), F32),
        compiler_params=_cparams("parallel", "parallel"),
    )(x, g, w_bf, *tabs)


def _stage_kv(k_rows, v_rows, kb_ref, vt_ref, nb):
    for n in range(nb):
        sl = slice(n * MOBA_BLOCK, (n + 1) * MOBA_BLOCK)
        kb_ref[n] = k_rows(sl).astype(BF16)
        vt_ref[n] = v_rows(sl).T.astype(BF16)


def _own_block(kb, vt, qs):
    st = lax.dot_general(kb, qs, _NT, preferred_element_type=F32)
    kpos = lax.broadcasted_iota(I32, st.shape, 0)
    qpos = lax.broadcasted_iota(I32, st.shape, 1)
    st = jnp.where(kpos <= qpos, st, NEG_INF)
    m = jnp.max(st, axis=0, keepdims=True)
    e = jnp.exp(st - m)
    l = jnp.sum(e, axis=0, keepdims=True)
    acc = jnp.dot(vt, e.astype(BF16), preferred_element_type=F32)
    return m, l, acc


def _past_block(carry, kb, vt, qs, keep):
    m, l, acc = carry
    st = lax.dot_general(kb, qs, _NT, preferred_element_type=F32)
    if keep is not None:
        st = jnp.where(keep, st, NEG_INF)
    m_new = jnp.maximum(m, jnp.max(st, axis=0, keepdims=True))
    c = jnp.exp(m - m_new)
    e = jnp.exp(st - m_new)
    l = l * c + jnp.sum(e, axis=0, keepdims=True)
    acc = acc * c + jnp.dot(vt, e.astype(BF16), preferred_element_type=F32)
    return m_new, l, acc


def _moba_prompt_kernel(q_ref, k_ref, v_ref, o_ref, kb_ref, vt_ref, kmean_ref, sel_ref, *, nb):
    i = pl.program_id(2)

    @pl.when(i == 0)
    def _():
        _stage_kv(lambda sl: k_ref[0, sl, :], lambda sl: v_ref[0, sl, :], kb_ref, vt_ref, nb)
        for n in range(nb):
            kmean_ref[n:n + 1, :] = jnp.mean(k_ref[0, n * MOBA_BLOCK:(n + 1) * MOBA_BLOCK, :],
                                             axis=0, keepdims=True)

    q = q_ref[0]
    lane = lax.broadcasted_iota(I32, q.shape, 1)
    blk = lax.broadcasted_iota(I32, (nb, q.shape[0]), 0)
    outs = []
    for h in range(LANES // HEAD_DIM):
        qh = jnp.where(lane // HEAD_DIM == h, q, 0.0)
        gt = lax.dot_general(kmean_ref[...], qh, _NT, precision=_HI, preferred_element_type=F32)
        gm = jnp.where(blk < i, gt, NEG_INF)
        beaten = jnp.zeros(gm.shape, I32)
        for jp in range(nb):
            row = gm[jp:jp + 1, :]
            beaten = beaten + ((row > gm) | ((row == gm) & (jp < blk))).astype(I32)
        sel_ref[h] = ((beaten < MOBA_TOPK) & (blk < i)).astype(F32)

        qs = (qh * SCALE).astype(BF16)
        vsl = slice(h * HEAD_DIM, (h + 1) * HEAD_DIM)
        carry = _own_block(kb_ref[i], vt_ref[i][vsl, :], qs)

        def body(j, carry, h=h, qs=qs, vsl=vsl):
            keep = sel_ref[h, pl.ds(j, 1), :] > 0.0
            return _past_block(carry, kb_ref[j], vt_ref[j][vsl, :], qs, keep)

        m, l, acc = lax.fori_loop(0, i, body, carry)
        outs.append(acc / l)
    o_ref[0] = jnp.concatenate(outs, axis=0).T


def _moba_prompt(qa, ka, va):
    b, s, _ = qa.shape
    nb = s // MOBA_BLOCK
    tq = MOBA_BLOCK
    return pl.pallas_call(
        functools.partial(_moba_prompt_kernel, nb=nb),
        grid=(b, WIDTH // LANES, nb),
        in_specs=[
            pl.BlockSpec((1, tq, LANES), lambda bb, hp, i: (bb, i, hp)),
            pl.BlockSpec((1, s, LANES), lambda bb, hp, i: (bb, 0, hp)),
            pl.BlockSpec((1, s, LANES), lambda bb, hp, i: (bb, 0, hp)),
        ],
        out_specs=pl.BlockSpec((1, tq, LANES), lambda bb, hp, i: (bb, i, hp)),
        out_shape=jax.ShapeDtypeStruct((b, s, WIDTH), F32),
        scratch_shapes=[
            pltpu.VMEM((nb, MOBA_BLOCK, LANES), BF16),
            pltpu.VMEM((nb, LANES, MOBA_BLOCK), BF16),
            pltpu.VMEM((nb, LANES), F32),
            pltpu.VMEM((LANES // HEAD_DIM, nb, tq), F32),
        ],
        compiler_params=_cparams("parallel", "parallel", "arbitrary"),
    )(qa, ka, va)


def _lambda_value(lq1_ref, lk1_ref, lq2_ref, lk2_ref, lam_init):
    a = jnp.sum(lq1_ref[...] * lk1_ref[...], axis=-1, keepdims=True)
    c = jnp.sum(lq2_ref[...] * lk2_ref[...], axis=-1, keepdims=True)
    return jnp.exp(a) - jnp.exp(c) + lam_init


def _diff_prompt_kernel(q_ref, k_ref, v_ref, lq1_ref, lk1_ref, lq2_ref, lk2_ref, gsub_ref,
                        o_ref, kb_ref, vt_ref, *, nb, lam_init):
    i = pl.program_id(2)

    @pl.when(i == 0)
    def _():
        _stage_kv(lambda sl: k_ref[0, 0, 0, sl, :], lambda sl: v_ref[0, 0, 0, sl, :], kb_ref, vt_ref, nb)

    lam = _lambda_value(lq1_ref, lk1_ref, lq2_ref, lk2_ref, lam_init)
    q = q_ref[0]
    lane = lax.broadcasted_iota(I32, q.shape, 1)
    maps = []
    for mp in range(2):
        qs = (jnp.where(lane // HEAD_DIM == mp, q, 0.0) * SCALE).astype(BF16)
        carry = _own_block(kb_ref[i], vt_ref[i], qs)

        def body(j, carry, qs=qs):
            return _past_block(carry, kb_ref[j], vt_ref[j], qs, None)

        m, l, acc = lax.fori_loop(0, i, body, carry)
        maps.append(acc / l)
    ot = maps[0] - lam * maps[1]
    ms = jnp.mean(ot * ot, axis=0, keepdims=True)
    y = ot * lax.rsqrt(ms + RMS_EPS) * gsub_ref[...] * (1.0 - lam_init)
    o_ref[0] = y.T


def _diff_prompt(qd, kd_hm, vd_hm, lams, gsub_col, lam_init):
    b, s, _ = qd.shape
    nb = s // MOBA_BLOCK
    tq = MOBA_BLOCK
    vec = lambda: pl.BlockSpec((1, HEAD_DIM), lambda bb, h, i: (0, 0))
    return pl.pallas_call(
        functools.partial(_diff_prompt_kernel, nb=nb, lam_init=lam_init),
        grid=(b, N_HEADS_DIFF, nb),
        in_specs=[
            pl.BlockSpec((1, tq, LANES), lambda bb, h, i: (bb, i, h)),
            pl.BlockSpec((1, 1, 1, s, LANES), lambda bb, h, i: (bb, 0, h, 0, 0)),
            pl.BlockSpec((1, 1, 1, s, LANES), lambda bb, h, i: (bb, 0, h, 0, 0)),
            vec(), vec(), vec(), vec(),
            pl.BlockSpec((LANES, 1), lambda bb, h, i: (0, 0)),
        ],
        out_specs=pl.BlockSpec((1, tq, LANES), lambda bb, h, i: (bb, i, h)),
        out_shape=jax.ShapeDtypeStruct((b, s, WIDTH), F32),
        scratch_shapes=[
            pltpu.VMEM((nb, MOBA_BLOCK, LANES), BF16),
            pltpu.VMEM((nb, LANES, MOBA_BLOCK), BF16),
        ],
        compiler_params=_cparams("parallel", "parallel", "arbitrary"),
    )(qd, kd_hm, vd_hm, *lams, gsub_col)


MOBA_GATE_PAGES = 8
DIFF_PAGES = 8


def _moba_gate_kernel(pt_ref, q_ref, *refs, n_pages):
    k_refs = refs[:MOBA_GATE_PAGES]
    sel_ref = refs[MOBA_GATE_PAGES]
    psum_ref = refs[MOBA_GATE_PAGES + 1]
    g = pl.program_id(1)
    ppb = MOBA_BLOCK // PAGE_SIZE
    for c in range(MOBA_GATE_PAGES):
        psum_ref[c % ppb, g * (MOBA_GATE_PAGES // ppb) + c // ppb] = jnp.sum(k_refs[c][0, 0], axis=1)

    @pl.when(g == pl.num_programs(1) - 1)
    def _():
        nb = n_pages // ppb
        kmean = (psum_ref[0] + psum_ref[1]) * (1.0 / MOBA_BLOCK)
        gate = jnp.sum(kmean * q_ref[0][None], axis=-1, keepdims=True)
        bidx = lax.broadcasted_iota(I32, gate.shape, 0)
        lane = lax.broadcasted_iota(I32, sel_ref.shape[1:], 1)
        out = jnp.zeros(sel_ref.shape[1:], I32)
        for r in range(MOBA_TOPK):
            m = jnp.max(gate, axis=0)
            idx = jnp.min(jnp.where(gate == m[None], bidx, nb), axis=0)
            out = jnp.where(lane == r, idx, out)
            gate = jnp.where(bidx == idx[None], -jnp.inf, gate)
        sel_ref[0] = out


def _moba_gate(pt_flat, q_heads, cache_k, n_pages):
    nbatch, heads, _ = q_heads.shape
    page_block = (1, 1, heads, PAGE_SIZE, HEAD_DIM)

    def page_spec(c):
        return pl.BlockSpec(page_block,
                            lambda b, g, pt: (pt[b * n_pages + g * MOBA_GATE_PAGES + c], 0, 0, 0, 0))

    grid_spec = pltpu.PrefetchScalarGridSpec(
        num_scalar_prefetch=1,
        grid=(nbatch, n_pages // MOBA_GATE_PAGES),
        in_specs=[pl.BlockSpec((1, heads, HEAD_DIM), lambda b, g, pt: (b, 0, 0))]
        + [page_spec(c) for c in range(MOBA_GATE_PAGES)],
        out_specs=pl.BlockSpec((1, heads, LANES), lambda b, g, pt: (b, 0, 0)),
        scratch_shapes=[pltpu.VMEM((MOBA_BLOCK // PAGE_SIZE, n_pages * PAGE_SIZE // MOBA_BLOCK,
                                    heads, HEAD_DIM), F32)],
    )
    return pl.pallas_call(
        functools.partial(_moba_gate_kernel, n_pages=n_pages),
        grid_spec=grid_spec,
        out_shape=jax.ShapeDtypeStruct((nbatch, heads, LANES), I32),
        compiler_params=_cparams("parallel", "arbitrary"),
    )(pt_flat, q_heads, *([cache_k] * MOBA_GATE_PAGES))


def _moba_sample_attn_kernel(ph_ref, q_ref, kn_ref, vn_ref, *refs, n_sel_pages):
    k_refs = refs[:n_sel_pages]
    v_refs = refs[n_sel_pages:2 * n_sel_pages]
    o_ref = refs[2 * n_sel_pages]
    q = q_ref[0, 0] * SCALE
    qs = jnp.broadcast_to(q, (8, HEAD_DIM)).astype(BF16)
    s_own = jnp.sum(q * kn_ref[0, 0], axis=-1, keepdims=True)
    s_pages = [lax.dot_general(qs, k_refs[c][0, 0, 0].astype(BF16), _NT, preferred_element_type=F32)
               for c in range(n_sel_pages)]
    m = s_own
    for s in s_pages:
        m = jnp.maximum(m, jnp.max(s, axis=-1, keepdims=True))
    e_own = jnp.exp(s_own - m)
    l = e_own
    acc = e_own * vn_ref[0, 0]
    for c in range(n_sel_pages):
        e = jnp.exp(s_pages[c] - m)
        l = l + jnp.sum(e, axis=-1, keepdims=True)
        acc = acc + jnp.dot(e.astype(BF16), v_refs[c][0, 0, 0].astype(BF16), preferred_element_type=F32)
    o_ref[0, 0] = (acc / l)[0:1]


def _moba_sample_attn(phys_flat, q4, kn4, vn4, cache_k, cache_v, n_sel_pages):
    nbatch, heads = q4.shape[:2]
    vec = lambda: pl.BlockSpec((1, 1, 1, HEAD_DIM), lambda b, h, ph: (b, h, 0, 0))
    slab = (1, 1, 1, PAGE_SIZE, HEAD_DIM)

    def slab_spec(c):
        return pl.BlockSpec(slab, lambda b, h, ph: (ph[(b * heads + h) * n_sel_pages + c], 0, h, 0, 0))

    grid_spec = pltpu.PrefetchScalarGridSpec(
        num_scalar_prefetch=1,
        grid=(nbatch, heads),
        in_specs=[vec(), vec(), vec()] + [slab_spec(c) for c in range(n_sel_pages)] * 2,
        out_specs=vec(),
    )
    return pl.pallas_call(
        functools.partial(_moba_sample_attn_kernel, n_sel_pages=n_sel_pages),
        grid_spec=grid_spec,
        out_shape=jax.ShapeDtypeStruct(q4.shape, F32),
        compiler_params=_cparams("parallel", "parallel"),
    )(phys_flat, q4, kn4, vn4, *([cache_k] * n_sel_pages), *([cache_v] * n_sel_pages))


def _diff_sample_kernel(pt_ref, q_ref, kn_ref, vn_ref, lq1_ref, lk1_ref, lq2_ref, lk2_ref, gsub_ref,
                        *refs, lam_init):
    k_refs = refs[:DIFF_PAGES]
    v_refs = refs[DIFF_PAGES:2 * DIFF_PAGES]
    o_ref, m_ref, l_ref, acc_ref = refs[2 * DIFF_PAGES:]
    g = pl.program_id(1)
    heads = q_ref.shape[1]
    rows = 16
    rmap = lax.broadcasted_iota(I32, (rows, LANES), 0) // 8
    lmap = lax.broadcasted_iota(I32, (rows, LANES), 1) // HEAD_DIM

    def stacked_q(h):
        q = q_ref[0, h] * SCALE
        return jnp.where(rmap == lmap, jnp.broadcast_to(q, (rows, LANES)), 0.0)

    @pl.when(g == 0)
    def _():
        for h in range(heads):
            s0 = jnp.sum(stacked_q(h) * kn_ref[0, h], axis=-1, keepdims=True)
            m_ref[h] = jnp.broadcast_to(s0, (rows, LANES))
            l_ref[h] = jnp.ones((rows, LANES), F32)
            acc_ref[h] = jnp.broadcast_to(vn_ref[0, h], (rows, LANES))

    for h in range(heads):
        qs = stacked_q(h).astype(BF16)
        m = m_ref[h]
        l = l_ref[h]
        acc = acc_ref[h]
        for c in range(DIFF_PAGES):
            st = lax.dot_general(qs, k_refs[c][0, 0, h].astype(BF16), _NT, preferred_element_type=F32)
            m_new = jnp.maximum(m, jnp.max(st, axis=-1, keepdims=True))
            corr = jnp.exp(m - m_new)
            e = jnp.exp(st - m_new)
            l = l * corr + jnp.sum(e, axis=-1, keepdims=True)
            acc = acc * corr + jnp.dot(e.astype(BF16), v_refs[c][0, 0, h].astype(BF16),
                                       preferred_element_type=F32)
            m = m_new
        m_ref[h] = m
        l_ref[h] = l
        acc_ref[h] = acc

    @pl.when(g == pl.num_programs(1) - 1)
    def _():
        lam = _lambda_value(lq1_ref, lk1_ref, lq2_ref, lk2_ref, lam_init)
        for h in range(heads):
            o = acc_ref[h] / l_ref[h]
            od = o[0:1] - lam * o[8:9]
            ms = jnp.mean(od * od, axis=-1, keepdims=True)
            o_ref[0, h] = od * lax.rsqrt(ms + RMS_EPS) * gsub_ref[...] * (1.0 - lam_init)


def _diff_sample(pt_flat, q4, kn4, vn4, lams, gsub_row, cache_k, cache_v, n_pages, lam_init):
    nbatch, heads = q4.shape[:2]
    tokv = lambda: pl.BlockSpec((1, heads, 1, LANES), lambda b, g, pt: (b, 0, 0, 0))
    vec = lambda: pl.BlockSpec((1, HEAD_DIM), lambda b, g, pt: (0, 0))
    page_block = (1, 1, heads, PAGE_SIZE, LANES)

    def page_spec(c):
        return pl.BlockSpec(page_block, lambda b, g, pt: (pt[b * n_pages + g * DIFF_PAGES + c], 0, 0, 0, 0))

    grid_spec = pltpu.PrefetchScalarGridSpec(
        num_scalar_prefetch=1,
        grid=(nbatch, n_pages // DIFF_PAGES),
        in_specs=[tokv(), tokv(), tokv(), vec(), vec(), vec(), vec(),
                  pl.BlockSpec((1, LANES), lambda b, g, pt: (0, 0))]
        + [page_spec(c) for c in range(DIFF_PAGES)] * 2,
        out_specs=tokv(),
        scratch_shapes=[pltpu.VMEM((heads, 16, LANES), F32)] * 3,
    )
    return pl.pallas_call(
        functools.partial(_diff_sample_kernel, lam_init=lam_init),
        grid_spec=grid_spec,
        out_shape=jax.ShapeDtypeStruct(q4.shape, F32),
        compiler_params=_cparams("parallel", "arbitrary"),
    )(pt_flat, q4, kn4, vn4, *lams, gsub_row, *([cache_k] * DIFF_PAGES), *([cache_v] * DIFF_PAGES))


def _outproj_router_kernel(x_ref, ma_ref, md_ref, wo_ref, g_ref, wr_ref, br_ref,
                           xres_ref, xn_ref, eidx_ref, gate_ref, rank_ref, cnt_ref, carry_ref):
    i = pl.program_id(0)

    @pl.when(i == 0)
    def _():
        carry_ref[...] = jnp.zeros(carry_ref.shape, F32)

    mix = jnp.concatenate([ma_ref[...], md_ref[...]], axis=1).astype(BF16)
    xr = x_ref[...] + jnp.dot(mix, wo_ref[...], preferred_element_type=F32)
    xres_ref[...] = xr
    xn = _norm_rows(xr, g_ref[...])
    xn_ref[...] = xn

    tm = xn.shape[0]
    ne = wr_ref.shape[0]
    lt = lax.dot_general(wr_ref[...], xn, _NT, precision=_HI, preferred_element_type=F32) + br_ref[...]
    eio = lax.broadcasted_iota(I32, lt.shape, 0)
    vals, idxs, sels = [], [], []
    for _ in range(TOP_K):
        m = jnp.max(lt, axis=0, keepdims=True)
        idx = jnp.min(jnp.where(lt == m, eio, ne), axis=0, keepdims=True)
        sel = eio == idx
        vals.append(m)
        idxs.append(idx)
        sels.append(sel)
        lt = jnp.where(sel, -jnp.inf, lt)
    es = [jnp.exp(v - vals[0]) for v in vals]
    tot = es[0] + es[1] + es[2] + es[3]
    gate_ref[...] = jnp.concatenate([e / tot for e in es], axis=0)
    eidx_ref[...] = jnp.concatenate(idxs, axis=0)

    onehot = jnp.zeros(lt.shape, F32)
    for sel in sels:
        onehot = onehot + sel.astype(F32)
    before = (lax.broadcasted_iota(I32, (tm, tm), 0) < lax.broadcasted_iota(I32, (tm, tm), 1)).astype(BF16)
    prefix = jnp.dot(onehot.astype(BF16), before, preferred_element_type=F32) + carry_ref[:, 0:1]
    ranks = [jnp.sum(jnp.where(sel, prefix, 0.0), axis=0, keepdims=True) for sel in sels]
    rank_ref[...] = jnp.concatenate(ranks, axis=0).astype(I32)
    carry_ref[...] = carry_ref[...] + jnp.sum(onehot, axis=1, keepdims=True)
    cnt_ref[...] = carry_ref[...].astype(I32)


def _outproj_router(x2, mix_a, mix_d, wo_bf, g_ffn, wr_t, br_col, tm):
    t, d = x2.shape
    ne = wr_t.shape[0]
    row = lambda w: pl.BlockSpec((tm, w), lambda i: (i, 0))
    full = lambda shp: pl.BlockSpec(shp, lambda i: (0, 0))
    small = lambda: pl.BlockSpec((TOP_K, tm), lambda i: (0, i))
    return pl.pallas_call(
        _outproj_router_kernel,
        grid=(t // tm,),
        in_specs=[row(d), row(WIDTH), row(WIDTH), full(wo_bf.shape), full((1, d)), full((ne, d)), full((ne, 1))],
        out_specs=[row(d), row(d), small(), small(), small(), full((ne, LANES))],
        out_shape=[
            jax.ShapeDtypeStruct((t, d), F32), jax.ShapeDtypeStruct((t, d), F32),
            jax.ShapeDtypeStruct((TOP_K, t), I32), jax.ShapeDtypeStruct((TOP_K, t), F32),
            jax.ShapeDtypeStruct((TOP_K, t), I32), jax.ShapeDtypeStruct((ne, LANES), I32),
        ],
        scratch_shapes=[pltpu.VMEM((ne, LANES), F32)],
        compiler_params=_cparams("arbitrary"),
    )(x2, mix_a, mix_d, wo_bf, g_ffn, wr_t, br_col)


def _dispatch_kernel(dest_ref, xn_ref, buf_in_ref, buf_ref, sem):
    del buf_in_ref
    i = pl.program_id(0)
    tm = xn_ref.shape[0]

    def row_copy(r, k):
        d = dest_ref[(i * tm + r) * TOP_K + k]
        return pltpu.make_async_copy(xn_ref.at[pl.ds(r, 1), :], buf_ref.at[pl.ds(d, 1), :], sem)

    def start(r, carry):
        for k in range(TOP_K):
            row_copy(r, k).start()
        return carry

    def wait(r, carry):
        for k in range(TOP_K):
            row_copy(r, k).wait()
        return carry

    lax.fori_loop(0, tm, start, 0)
    lax.fori_loop(0, tm, wait, 0)


def _dispatch(dest_flat, xn, buf, tm):
    t, d = xn.shape
    grid_spec = pltpu.PrefetchScalarGridSpec(
        num_scalar_prefetch=1,
        grid=(t // tm,),
        in_specs=[pl.BlockSpec((tm, d), lambda i, dest: (i, 0)), pl.BlockSpec(memory_space=pl.ANY)],
        out_specs=pl.BlockSpec(memory_space=pl.ANY),
        scratch_shapes=[pltpu.SemaphoreType.DMA(())],
    )
    return pl.pallas_call(
        _dispatch_kernel,
        grid_spec=grid_spec,
        out_shape=jax.ShapeDtypeStruct(buf.shape, buf.dtype),
        input_output_aliases={2: 0},
        compiler_params=_cparams("arbitrary"),
    )(dest_flat, xn, buf)


def _experts_kernel(be_ref, nu_ref, x_ref, wg_ref, bg_ref, wu_ref, bu_ref, wd_ref, bd_ref, y_ref):
    i = pl.program_id(0)

    @pl.when(i < nu_ref[0])
    def _():
        x = x_ref[...].astype(BF16)
        g = jnp.dot(x, wg_ref[0], preferred_element_type=F32) + bg_ref[0]
        u = jnp.dot(x, wu_ref[0], preferred_element_type=F32) + bu_ref[0]
        g = jnp.minimum(g, SWIGLU_LIMIT)
        u = jnp.clip(u, -SWIGLU_LIMIT, SWIGLU_LIMIT)
        h = (g * jax.nn.sigmoid(SWIGLU_ALPHA * g) * (u + 1.0)).astype(BF16)
        y_ref[...] = jnp.dot(h, wd_ref[0], preferred_element_type=F32) + bd_ref[0]

    @pl.when(i >= nu_ref[0])
    def _():
        y_ref[...] = jnp.zeros(y_ref.shape, F32)


def _experts(block_e, n_used, buf, wg, bg, wu, bu, wd, bd):
    n_rows, d = buf.shape
    dff = wg.shape[2]
    wspec = lambda shp: pl.BlockSpec((1,) + shp, lambda i, be, nu: (be[i], 0, 0))
    grid_spec = pltpu.PrefetchScalarGridSpec(
        num_scalar_prefetch=2,
        grid=(n_rows // MOE_ROWS,),
        in_specs=[pl.BlockSpec((MOE_ROWS, d), lambda i, be, nu: (i, 0)),
                  wspec((d, dff)), wspec((1, dff)), wspec((d, dff)), wspec((1, dff)),
                  wspec((dff, d)), wspec((1, d))],
        out_specs=pl.BlockSpec((MOE_ROWS, d), lambda i, be, nu: (i, 0)),
    )
    return pl.pallas_call(
        _experts_kernel,
        grid_spec=grid_spec,
        out_shape=jax.ShapeDtypeStruct((n_rows, d), F32),
        compiler_params=_cparams("arbitrary"),
    )(block_e, n_used, buf, wg, bg, wu, bu, wd, bd)


def _combine_kernel(dest_ref, xres_ref, gate_ref, gfin_ref, yb_ref, out_ref, rows_ref, sem):
    i = pl.program_id(0)
    tm = xres_ref.shape[0]

    def row_copy(r, k):
        d = dest_ref[(i * tm + r) * TOP_K + k]
        return pltpu.make_async_copy(yb_ref.at[pl.ds(d, 1), :], rows_ref.at[k, pl.ds(r, 1), :], sem)

    def start(r, carry):
        for k in range(TOP_K):
            row_copy(r, k).start()
        return carry

    def wait(r, carry):
        for k in range(TOP_K):
            row_copy(r, k).wait()
        return carry

    lax.fori_loop(0, tm, start, 0)
    lax.fori_loop(0, tm, wait, 0)
    gates = gate_ref[...]
    y = gates[:, 0:1] * rows_ref[0]
    for k in range(1, TOP_K):
        y = y + gates[:, k:k + 1] * rows_ref[k]
    out_ref[...] = _norm_rows(xres_ref[...] + y, gfin_ref[...])


def _combine(dest_flat, xres, gates_col, g_final, yb, tm):
    t, d = xres.shape
    grid_spec = pltpu.PrefetchScalarGridSpec(
        num_scalar_prefetch=1,
        grid=(t // tm,),
        in_specs=[pl.BlockSpec((tm, d), lambda i, dest: (i, 0)),
                  pl.BlockSpec((tm, TOP_K), lambda i, dest: (i, 0)),
                  pl.BlockSpec((1, d), lambda i, dest: (0, 0)),
                  pl.BlockSpec(memory_space=pl.ANY)],
        out_specs=pl.BlockSpec((tm, d), lambda i, dest: (i, 0)),
        scratch_shapes=[pltpu.VMEM((TOP_K, tm, d), F32), pltpu.SemaphoreType.DMA(())],
    )
    return pl.pallas_call(
        _combine_kernel,
        grid_spec=grid_spec,
        out_shape=jax.ShapeDtypeStruct((t, d), F32),
        compiler_params=_cparams("arbitrary"),
    )(dest_flat, xres, gates_col, g_final, yb)


def kernel(x_prompt, x_sample, cache_k_moba, cache_v_moba, cache_k_diff, cache_v_diff, page_table,
           g_mix, w_in, lambda_q1, lambda_k1, lambda_q2, lambda_k2, g_subln, w_out, g_ffn,
           w_router, b_router, w_gate, b_gate, w_up, b_up, w_down, b_down, g_final):
    depth = w_in.shape[0]
    assert depth == 1, "single-layer step"
    batch, seq, d_model = x_prompt.shape
    dec_batch, dec_seq, _ = x_sample.shape
    assert dec_seq == 1
    n_pages = page_table.shape[1]
    n_experts = w_router.shape[2]
    layer = 0
    lam_init = 0.8 - 0.6 * math.exp(-0.3 * layer)
    ppb = MOBA_BLOCK // PAGE_SIZE
    assert n_pages % ppb == 0 and n_pages // ppb >= MOBA_TOPK

    w_in_bf = w_in[layer].astype(BF16)
    w_out_bf = w_out[layer].astype(BF16)
    g_mix_row = g_mix[layer][None]
    lams = (lambda_q1[layer][None], lambda_k1[layer][None], lambda_q2[layer][None], lambda_k2[layer][None])
    pt_flat = page_table.reshape(-1)

    tabs_p = _rope_tables(jnp.arange(seq, dtype=I32))
    qa, ka, va, qd, ka_hm, va_hm, kd_hm, vd_hm = _inproj_prompt(x_prompt, g_mix_row, w_in_bf, tabs_p, tm=256)
    mix_a_p = _moba_prompt(qa, ka, va)
    mix_d_p = _diff_prompt(qd, kd_hm, vd_hm, lams, g_subln[layer][:, None], lam_init)

    pos_s = n_pages * PAGE_SIZE + jnp.zeros((dec_batch,), I32)
    tabs_s = _rope_tables(pos_s)
    p_s = _inproj_sample(x_sample.reshape(1, dec_batch, d_model), g_mix_row, w_in_bf, tabs_s)[0]
    qa_s, ka_s, va_s, qd_s, kd_s, vd_s = (p_s[:, s * WIDTH:(s + 1) * WIDTH] for s in range(6))
    heads4 = lambda a, h: a.reshape(dec_batch, h, 1, WIDTH // h)
    sel = _moba_gate(pt_flat, qa_s.reshape(dec_batch, N_HEADS_MOBA, HEAD_DIM), cache_k_moba, n_pages)
    sel = sel[:, :, :MOBA_TOPK]
    logical = sel[..., None] * ppb + jnp.arange(ppb, dtype=I32)
    phys = jnp.take_along_axis(page_table, logical.reshape(dec_batch, -1), axis=1)
    mix_a_s = _moba_sample_attn(phys.reshape(-1), heads4(qa_s, N_HEADS_MOBA), heads4(ka_s, N_HEADS_MOBA),
                                heads4(va_s, N_HEADS_MOBA), cache_k_moba, cache_v_moba, MOBA_TOPK * ppb)
    mix_d_s = _diff_sample(pt_flat, heads4(qd_s, N_HEADS_DIFF), heads4(kd_s, N_HEADS_DIFF),
                           heads4(vd_s, N_HEADS_DIFF), lams, g_subln[layer][None],
                           cache_k_diff, cache_v_diff, n_pages, lam_init)

    wr_t = w_router[layer].T
    br_col = b_router[layer][:, None]
    g_ffn_row = g_ffn[layer][None]
    t_p = batch * seq
    xres_p, xn_p, eidx_p, gate_p, rank_p, cnt_p = _outproj_router(
        x_prompt.reshape(t_p, d_model), mix_a_p.reshape(t_p, WIDTH), mix_d_p.reshape(t_p, WIDTH),
        w_out_bf, g_ffn_row, wr_t, br_col, tm=512)
    xres_s, xn_s, eidx_s, gate_s, rank_s, cnt_s = _outproj_router(
        x_sample.reshape(dec_batch, d_model), mix_a_s.reshape(dec_batch, WIDTH),
        mix_d_s.reshape(dec_batch, WIDTH), w_out_bf, g_ffn_row, wr_t, br_col, tm=dec_batch)

    cnt_p = cnt_p[:, 0]
    sizes = cnt_p + cnt_s[:, 0]
    padded = (sizes + MOE_ROWS - 1) // MOE_ROWS * MOE_ROWS
    pends = jnp.cumsum(padded)
    pstart = pends - padded
    n_assign = (t_p + dec_batch) * TOP_K
    n_blocks = -(-(n_assign + n_experts * (MOE_ROWS - 1)) // MOE_ROWS)
    dest_p = (pstart[eidx_p] + rank_p).T.reshape(-1)
    dest_s = (pstart[eidx_s] + cnt_p[eidx_s] + rank_s).T.reshape(-1)
    block_e = jnp.minimum(jnp.searchsorted(pends, jnp.arange(n_blocks, dtype=I32) * MOE_ROWS, side='right'),
                          n_experts - 1).astype(I32)
    n_used = (pends[-1:] // MOE_ROWS).astype(I32)

    buf = jnp.zeros((n_blocks * MOE_ROWS, d_model), F32)
    buf = _dispatch(dest_p, xn_p, buf, tm=512)
    buf = _dispatch(dest_s, xn_s, buf, tm=dec_batch)
    yb = _experts(block_e, n_used, buf,
                  w_gate[layer].astype(BF16), b_gate[layer][:, None, :],
                  w_up[layer].astype(BF16), b_up[layer][:, None, :],
                  w_down[layer].astype(BF16), b_down[layer][:, None, :])
    g_fin_row = g_final[None]
    y_p = _combine(dest_p, xres_p, gate_p.T, g_fin_row, yb, tm=256)
    y_s = _combine(dest_s, xres_s, gate_s.T, g_fin_row, yb, tm=dec_batch)

    y_prompt = y_p.reshape(batch, seq, d_model)
    y_sample = y_s.reshape(dec_batch, 1, d_model)
    k_moba_sample = ka_s.reshape(dec_batch, 1, N_HEADS_MOBA, 1, HEAD_DIM)
    v_moba_sample = va_s.reshape(dec_batch, 1, N_HEADS_MOBA, 1, HEAD_DIM)
    k_diff_sample = kd_s.reshape(dec_batch, 1, N_HEADS_DIFF, 1, LANES)
    v_diff_sample = vd_s.reshape(dec_batch, 1, N_HEADS_DIFF, 1, LANES)
    return (y_prompt, y_sample, ka_hm, va_hm, kd_hm, vd_hm,
            k_moba_sample, v_moba_sample, k_diff_sample, v_diff_sample)
```

```python
import functools
import math

import jax
import jax.numpy as jnp
from jax import lax
from jax.experimental import pallas as pl
from jax.experimental.pallas import tpu as pltpu

F32 = jnp.float32
BF16 = jnp.bfloat16
I32 = jnp.int32

HEAD_DIM = 64
N_HEADS_MOBA = 8
N_HEADS_DIFF = 4
WIDTH = 512
ROPE_THETA = 500000.0
ROT_DIM = HEAD_DIM // 4
MOBA_BLOCK = 256
MOBA_TOPK = 3
PAGE_SIZE = 128
TOP_K = 4
SWIGLU_LIMIT = 7.0
SWIGLU_ALPHA = 1.702
RMS_EPS = 1e-5
NEG_INF = -1e30
SCALE = HEAD_DIM ** -0.5

LANES = 128
MOE_ROWS = 256
VMEM_LIMIT = 48 * 1024 * 1024

_NT = (((1,), (1,)), ((), ()))
_HI = lax.Precision.HIGHEST


def _cparams(*sem):
    return pltpu.CompilerParams(dimension_semantics=sem, vmem_limit_bytes=VMEM_LIMIT)


def _rope_tables(pos):
    half = ROT_DIM // 2
    inv = ROPE_THETA ** (-jnp.arange(half, dtype=F32) * (2.0 / ROT_DIM))
    ang = pos.astype(F32)[:, None] * inv[None, :]
    cos = jnp.cos(ang)
    sin = jnp.sin(ang)
    s = pos.shape[0]
    pad = HEAD_DIM - ROT_DIM
    c64 = jnp.concatenate([cos, cos, jnp.ones((s, pad), F32)], axis=1)
    a64 = jnp.concatenate([-sin, jnp.zeros((s, half + pad), F32)], axis=1)
    b64 = jnp.concatenate([jnp.zeros((s, half), F32), sin, jnp.zeros((s, pad), F32)], axis=1)
    rep = LANES // HEAD_DIM
    return jnp.tile(c64, (1, rep)), jnp.tile(a64, (1, rep)), jnp.tile(b64, (1, rep))


def _norm_rows(x, g):
    ms = jnp.mean(x * x, axis=-1, keepdims=True)
    return x * lax.rsqrt(ms + RMS_EPS) * g


def _rope_section(p, c, sa, sb):
    half = ROT_DIM // 2
    outs = []
    for j in range(p.shape[1] // LANES):
        xc = p[:, j * LANES:(j + 1) * LANES]
        outs.append(xc * c + pltpu.roll(xc, LANES - half, 1) * sa + pltpu.roll(xc, half, 1) * sb)
    return jnp.concatenate(outs, axis=1)


def _inproj_sections(x_ref, g_ref, w_ref, c_ref, sa_ref, sb_ref):
    xn = _norm_rows(x_ref[0], g_ref[...]).astype(BF16)
    c, sa, sb = c_ref[...], sa_ref[...], sb_ref[...]
    secs = []
    for s in range(6):
        p = jnp.dot(xn, w_ref[:, s * WIDTH:(s + 1) * WIDTH], preferred_element_type=F32)
        if s in (0, 1, 3, 4):
            p = _rope_section(p, c, sa, sb)
        secs.append(p)
    return secs


def _inproj_prompt_kernel(x_ref, g_ref, w_ref, c_ref, sa_ref, sb_ref,
                          qa_ref, ka_ref, va_ref, qd_ref, kahm_ref, vahm_ref, kdhm_ref, vdhm_ref):
    qa, ka, va, qd, kd, vd = _inproj_sections(x_ref, g_ref, w_ref, c_ref, sa_ref, sb_ref)
    qa_ref[0] = qa
    ka_ref[0] = ka
    va_ref[0] = va
    qd_ref[0] = qd
    for h in range(N_HEADS_MOBA):
        kahm_ref[0, 0, h] = ka[:, h * HEAD_DIM:(h + 1) * HEAD_DIM]
        vahm_ref[0, 0, h] = va[:, h * HEAD_DIM:(h + 1) * HEAD_DIM]
    for h in range(N_HEADS_DIFF):
        kdhm_ref[0, 0, h] = kd[:, h * LANES:(h + 1) * LANES]
        vdhm_ref[0, 0, h] = vd[:, h * LANES:(h + 1) * LANES]


def _inproj_sample_kernel(x_ref, g_ref, w_ref, c_ref, sa_ref, sb_ref, p_ref):
    secs = _inproj_sections(x_ref, g_ref, w_ref, c_ref, sa_ref, sb_ref)
    for s in range(6):
        p_ref[0, :, s * WIDTH:(s + 1) * WIDTH] = secs[s]


def _inproj_in_specs(tm, d, pw):
    return [
        pl.BlockSpec((1, tm, d), lambda b, i: (b, i, 0)),
        pl.BlockSpec((1, d), lambda b, i: (0, 0)),
        pl.BlockSpec((d, pw), lambda b, i: (0, 0)),
        pl.BlockSpec((tm, LANES), lambda b, i: (i, 0)),
        pl.BlockSpec((tm, LANES), lambda b, i: (i, 0)),
        pl.BlockSpec((tm, LANES), lambda b, i: (i, 0)),
    ]


def _inproj_prompt(x, g, w_bf, tabs, tm):
    b, s, d = x.shape
    pw = w_bf.shape[1]
    tok = lambda: pl.BlockSpec((1, tm, WIDTH), lambda bb, i: (bb, i, 0))
    hm_a = lambda: pl.BlockSpec((1, 1, N_HEADS_MOBA, tm, HEAD_DIM), lambda bb, i: (bb, 0, 0, i, 0))
    hm_d = lambda: pl.BlockSpec((1, 1, N_HEADS_DIFF, tm, LANES), lambda bb, i: (bb, 0, 0, i, 0))
    tok_shape = jax.ShapeDtypeStruct((b, s, WIDTH), F32)
    hma_shape = jax.ShapeDtypeStruct((b, 1, N_HEADS_MOBA, s, HEAD_DIM), F32)
    hmd_shape = jax.ShapeDtypeStruct((b, 1, N_HEADS_DIFF, s, LANES), F32)
    return pl.pallas_call(
        _inproj_prompt_kernel,
        grid=(b, s // tm),
        in_specs=_inproj_in_specs(tm, d, pw),
        out_specs=[tok(), tok(), tok(), tok(), hm_a(), hm_a(), hm_d(), hm_d()],
        out_shape=[tok_shape, tok_shape, tok_shape, tok_shape, hma_shape, hma_shape, hmd_shape, hmd_shape],
        compiler_params=_cparams("parallel", "parallel"),
        name="inproj_prompt",
    )(x, g, w_bf, *tabs)


def _inproj_sample(x, g, w_bf, tabs):
    b, s, d = x.shape
    pw = w_bf.shape[1]
    return pl.pallas_call(
        _inproj_sample_kernel,
        grid=(b, 1),
        in_specs=_inproj_in_specs(s, d, pw),
        out_specs=pl.BlockSpec((1, s, pw), lambda bb, i: (bb, 0, 0)),
        out_shape=jax.ShapeDtypeStruct((b, s, pw), F32),
        compiler_params=_cparams("parallel", "parallel"),
        name="inproj_sample",
    )(x, g, w_bf, *tabs)


def _stage_kv(k_rows, v_rows, kb_ref, vt_ref, nb):
    for n in range(nb):
        sl = slice(n * MOBA_BLOCK, (n + 1) * MOBA_BLOCK)
        kb_ref[n] = k_rows(sl).astype(BF16)
        vt_ref[n] = v_rows(sl).T.astype(BF16)


def _own_block(kb, vt, qs):
    st = lax.dot_general(kb, qs, _NT, preferred_element_type=F32)
    kpos = lax.broadcasted_iota(I32, st.shape, 0)
    qpos = lax.broadcasted_iota(I32, st.shape, 1)
    st = jnp.where(kpos <= qpos, st, NEG_INF)
    m = jnp.max(st, axis=0, keepdims=True)
    e = jnp.exp(st - m)
    l = jnp.sum(e, axis=0, keepdims=True)
    acc = jnp.dot(vt, e.astype(BF16), preferred_element_type=F32)
    return m, l, acc


def _past_block(carry, kb, vt, qs, keep):
    m, l, acc = carry
    st = lax.dot_general(kb, qs, _NT, preferred_element_type=F32)
    if keep is not None:
        st = jnp.where(keep, st, NEG_INF)
    m_new = jnp.maximum(m, jnp.max(st, axis=0, keepdims=True))
    c = jnp.exp(m - m_new)
    e = jnp.exp(st - m_new)
    l = l * c + jnp.sum(e, axis=0, keepdims=True)
    acc = acc * c + jnp.dot(vt, e.astype(BF16), preferred_element_type=F32)
    return m_new, l, acc


def _moba_prompt_kernel(q_ref, k_ref, v_ref, o_ref, kb_ref, vt_ref, kmean_ref, sel_ref, *, nb):
    i = pl.program_id(2)

    @pl.when(i == 0)
    def _():
        _stage_kv(lambda sl: k_ref[0, sl, :], lambda sl: v_ref[0, sl, :], kb_ref, vt_ref, nb)
        for n in range(nb):
            kmean_ref[n:n + 1, :] = jnp.mean(k_ref[0, n * MOBA_BLOCK:(n + 1) * MOBA_BLOCK, :],
                                             axis=0, keepdims=True)

    q = q_ref[0]
    lane = lax.broadcasted_iota(I32, q.shape, 1)
    blk = lax.broadcasted_iota(I32, (nb, q.shape[0]), 0)
    n_heads = LANES // HEAD_DIM
    vsl = [slice(h * HEAD_DIM, (h + 1) * HEAD_DIM) for h in range(n_heads)]
    qs = []
    for h in range(n_heads):
        qh = jnp.where(lane // HEAD_DIM == h, q, 0.0)
        gt = lax.dot_general(kmean_ref[...], qh, _NT, precision=_HI, preferred_element_type=F32)
        gm = jnp.where(blk < i, gt, NEG_INF)
        beaten = jnp.zeros(gm.shape, I32)
        for jp in range(nb):
            row = gm[jp:jp + 1, :]
            beaten = beaten + ((row > gm) | ((row == gm) & (jp < blk))).astype(I32)
        sel_ref[h] = ((beaten < MOBA_TOPK) & (blk < i)).astype(F32)
        qs.append((qh * SCALE).astype(BF16))

    kb_own = kb_ref[i]
    carry = tuple(_own_block(kb_own, vt_ref[i, vsl[h], :], qs[h]) for h in range(n_heads))

    def body(j, carry):
        kb = kb_ref[j]
        return tuple(_past_block(carry[h], kb, vt_ref[j, vsl[h], :], qs[h],
                                 sel_ref[h, pl.ds(j, 1), :] > 0.0) for h in range(n_heads))

    carry = lax.fori_loop(0, i, body, carry)
    o_ref[0] = jnp.concatenate([acc / l for _, l, acc in carry], axis=0).T


def _moba_prompt(qa, ka, va):
    b, s, _ = qa.shape
    nb = s // MOBA_BLOCK
    tq = MOBA_BLOCK
    return pl.pallas_call(
        functools.partial(_moba_prompt_kernel, nb=nb),
        grid=(b, WIDTH // LANES, nb),
        in_specs=[
            pl.BlockSpec((1, tq, LANES), lambda bb, hp, i: (bb, i, hp)),
            pl.BlockSpec((1, s, LANES), lambda bb, hp, i: (bb, 0, hp)),
            pl.BlockSpec((1, s, LANES), lambda bb, hp, i: (bb, 0, hp)),
        ],
        out_specs=pl.BlockSpec((1, tq, LANES), lambda bb, hp, i: (bb, i, hp)),
        out_shape=jax.ShapeDtypeStruct((b, s, WIDTH), F32),
        scratch_shapes=[
            pltpu.VMEM((nb, MOBA_BLOCK, LANES), BF16),
            pltpu.VMEM((nb, LANES, MOBA_BLOCK), BF16),
            pltpu.VMEM((nb, LANES), F32),
            pltpu.VMEM((LANES // HEAD_DIM, nb, tq), F32),
        ],
        compiler_params=_cparams("parallel", "parallel", "arbitrary"),
        name="moba_prompt_attn",
    )(qa, ka, va)


def _lambda_value(lq1_ref, lk1_ref, lq2_ref, lk2_ref, lam_init):
    a = jnp.sum(lq1_ref[...] * lk1_ref[...], axis=-1, keepdims=True)
    c = jnp.sum(lq2_ref[...] * lk2_ref[...], axis=-1, keepdims=True)
    return jnp.exp(a) - jnp.exp(c) + lam_init


def _diff_prompt_kernel(q_ref, k_ref, v_ref, lq1_ref, lk1_ref, lq2_ref, lk2_ref, gsub_ref,
                        o_ref, kb_ref, vt_ref, *, nb, lam_init):
    i = pl.program_id(2)

    @pl.when(i == 0)
    def _():
        _stage_kv(lambda sl: k_ref[0, 0, 0, sl, :], lambda sl: v_ref[0, 0, 0, sl, :], kb_ref, vt_ref, nb)

    lam = _lambda_value(lq1_ref, lk1_ref, lq2_ref, lk2_ref, lam_init)
    q = q_ref[0]
    lane = lax.broadcasted_iota(I32, q.shape, 1)
    qs = [(jnp.where(lane // HEAD_DIM == mp, q, 0.0) * SCALE).astype(BF16) for mp in range(2)]
    kb_own, vt_own = kb_ref[i], vt_ref[i]
    carry = tuple(_own_block(kb_own, vt_own, qs[mp]) for mp in range(2))

    def body(j, carry):
        kb, vt = kb_ref[j], vt_ref[j]
        return tuple(_past_block(carry[mp], kb, vt, qs[mp], None) for mp in range(2))

    carry = lax.fori_loop(0, i, body, carry)
    maps = [acc / l for _, l, acc in carry]
    ot = maps[0] - lam * maps[1]
    ms = jnp.mean(ot * ot, axis=0, keepdims=True)
    y = ot * lax.rsqrt(ms + RMS_EPS) * gsub_ref[...] * (1.0 - lam_init)
    o_ref[0] = y.T


def _diff_prompt(qd, kd_hm, vd_hm, lams, gsub_col, lam_init):
    b, s, _ = qd.shape
    nb = s // MOBA_BLOCK
    tq = MOBA_BLOCK
    vec = lambda: pl.BlockSpec((1, HEAD_DIM), lambda bb, h, i: (0, 0))
    return pl.pallas_call(
        functools.partial(_diff_prompt_kernel, nb=nb, lam_init=lam_init),
        grid=(b, N_HEADS_DIFF, nb),
        in_specs=[
            pl.BlockSpec((1, tq, LANES), lambda bb, h, i: (bb, i, h)),
            pl.BlockSpec((1, 1, 1, s, LANES), lambda bb, h, i: (bb, 0, h, 0, 0)),
            pl.BlockSpec((1, 1, 1, s, LANES), lambda bb, h, i: (bb, 0, h, 0, 0)),
            vec(), vec(), vec(), vec(),
            pl.BlockSpec((LANES, 1), lambda bb, h, i: (0, 0)),
        ],
        out_specs=pl.BlockSpec((1, tq, LANES), lambda bb, h, i: (bb, i, h)),
        out_shape=jax.ShapeDtypeStruct((b, s, WIDTH), F32),
        scratch_shapes=[
            pltpu.VMEM((nb, MOBA_BLOCK, LANES), BF16),
            pltpu.VMEM((nb, LANES, MOBA_BLOCK), BF16),
        ],
        compiler_params=_cparams("parallel", "parallel", "arbitrary"),
        name="diff_prompt_attn",
    )(qd, kd_hm, vd_hm, *lams, gsub_col)


MOBA_GATE_PAGES = 16
DIFF_PAGES = 8


def _moba_gate_kernel(pt_ref, q_ref, *refs, n_pages):
    k_refs = refs[:MOBA_GATE_PAGES]
    sel_ref = refs[MOBA_GATE_PAGES]
    part_ref = refs[MOBA_GATE_PAGES + 1]
    g = pl.program_id(1)
    ppb = MOBA_BLOCK // PAGE_SIZE
    heads, hd = q_ref.shape[1:3]
    qb = jnp.broadcast_to(q_ref[0], (heads, hd, PAGE_SIZE))
    for n in range(MOBA_GATE_PAGES // ppb):
        ksum = k_refs[n * ppb][0, 0]
        for c in range(1, ppb):
            ksum = ksum + k_refs[n * ppb + c][0, 0]
        part_ref[g * (MOBA_GATE_PAGES // ppb) + n] = jnp.sum(ksum * qb, axis=1)

    @pl.when(g == pl.num_programs(1) - 1)
    def _():
        nb = n_pages // ppb
        gate = jnp.sum(part_ref[...], axis=-1, keepdims=True) * (1.0 / MOBA_BLOCK)
        bidx = lax.broadcasted_iota(I32, gate.shape, 0)
        lane = lax.broadcasted_iota(I32, sel_ref.shape[1:], 1)
        out = jnp.zeros(sel_ref.shape[1:], I32)
        for r in range(MOBA_TOPK):
            m = jnp.max(gate, axis=0)
            idx = jnp.min(jnp.where(gate == m[None], bidx, nb), axis=0)
            out = jnp.where(lane == r, idx, out)
            gate = jnp.where(bidx == idx[None], -jnp.inf, gate)
        sel_ref[0] = out


def _moba_gate(pt_flat, q_cols, cache_kt, n_pages):
    nbatch, heads, hd, _ = q_cols.shape
    page_block = (1, 1, heads, hd, PAGE_SIZE)

    def page_spec(c):
        return pl.BlockSpec(page_block,
                            lambda b, g, pt: (pt[b * n_pages + g * MOBA_GATE_PAGES + c], 0, 0, 0, 0))

    grid_spec = pltpu.PrefetchScalarGridSpec(
        num_scalar_prefetch=1,
        grid=(nbatch, n_pages // MOBA_GATE_PAGES),
        in_specs=[pl.BlockSpec((1, heads, hd, 1), lambda b, g, pt: (b, 0, 0, 0))]
        + [page_spec(c) for c in range(MOBA_GATE_PAGES)],
        out_specs=pl.BlockSpec((1, heads, LANES), lambda b, g, pt: (b, 0, 0)),
        scratch_shapes=[pltpu.VMEM((n_pages * PAGE_SIZE // MOBA_BLOCK, heads, PAGE_SIZE), F32)],
    )
    return pl.pallas_call(
        functools.partial(_moba_gate_kernel, n_pages=n_pages),
        grid_spec=grid_spec,
        out_shape=jax.ShapeDtypeStruct((nbatch, heads, LANES), I32),
        compiler_params=_cparams("parallel", "arbitrary"),
        name="moba_sample_gate",
    )(pt_flat, q_cols, *([cache_kt] * MOBA_GATE_PAGES))


def _moba_sample_attn_kernel(ph_ref, q_ref, kn_ref, vn_ref, kt_hbm, vt_hbm, o_ref,
                             kbuf_ref, vbuf_ref, sem, *, n_sel_pages, layer):
    b = pl.program_id(0)
    heads = q_ref.shape[1]
    slot = b % 2

    def slab_copies(seq, slot):
        copies = []
        for h in range(heads):
            for c in range(n_sel_pages):
                j = h * n_sel_pages + c
                page = ph_ref[seq * (heads * n_sel_pages) + j]
                copies.append(pltpu.make_async_copy(kt_hbm.at[page, layer, h], kbuf_ref.at[slot, j], sem.at[slot]))
                copies.append(pltpu.make_async_copy(vt_hbm.at[page, layer, h], vbuf_ref.at[slot, j], sem.at[slot]))
        return copies

    @pl.when(b == 0)
    def _():
        for cp in slab_copies(b, slot):
            cp.start()

    @pl.when(b + 1 < pl.num_programs(0))
    def _():
        for cp in slab_copies(b + 1, 1 - slot):
            cp.start()

    for cp in slab_copies(b, slot):
        cp.wait()

    for h in range(heads):
        q = q_ref[0, h] * SCALE
        qs = jnp.broadcast_to(q, (8, HEAD_DIM)).astype(BF16)
        s_own = jnp.sum(q * kn_ref[0, h], axis=-1, keepdims=True)
        kts = [kbuf_ref[slot, h * n_sel_pages + c].astype(BF16) for c in range(n_sel_pages)]
        s_pages = [jnp.dot(qs, kt, preferred_element_type=F32) for kt in kts]
        m = s_own
        for s in s_pages:
            m = jnp.maximum(m, jnp.max(s, axis=-1, keepdims=True))
        e_own = jnp.exp(s_own - m)
        l = e_own
        acc = e_own * vn_ref[0, h]
        for c in range(n_sel_pages):
            e = jnp.exp(s_pages[c] - m)
            l = l + jnp.sum(e, axis=-1, keepdims=True)
            vt = vbuf_ref[slot, h * n_sel_pages + c].astype(BF16)
            acc = acc + lax.dot_general(e.astype(BF16), vt, _NT, preferred_element_type=F32)
        o_ref[0, h] = (acc / l)[0:1]


def _moba_sample_attn(phys_flat, q4, kn4, vn4, cache_kt, cache_vt, n_sel_pages, layer):
    nbatch, heads = q4.shape[:2]
    vec = lambda: pl.BlockSpec((1, heads, 1, HEAD_DIM), lambda b, ph: (b, 0, 0, 0))
    n_slabs = heads * n_sel_pages
    grid_spec = pltpu.PrefetchScalarGridSpec(
        num_scalar_prefetch=1,
        grid=(nbatch,),
        in_specs=[vec(), vec(), vec(), pl.BlockSpec(memory_space=pl.ANY), pl.BlockSpec(memory_space=pl.ANY)],
        out_specs=vec(),
        scratch_shapes=[pltpu.VMEM((2, n_slabs, HEAD_DIM, PAGE_SIZE), F32),
                        pltpu.VMEM((2, n_slabs, HEAD_DIM, PAGE_SIZE), F32),
                        pltpu.SemaphoreType.DMA((2,))],
    )
    return pl.pallas_call(
        functools.partial(_moba_sample_attn_kernel, n_sel_pages=n_sel_pages, layer=layer),
        grid_spec=grid_spec,
        out_shape=jax.ShapeDtypeStruct(q4.shape, F32),
        compiler_params=_cparams("arbitrary"),
        name="moba_sample_attn",
    )(phys_flat, q4, kn4, vn4, cache_kt, cache_vt)


def _diff_rows_kernel(pt_ref, q_ref, kn_ref, vn_ref, lq1_ref, lk1_ref, lq2_ref, lk2_ref, gsub_ref,
                      *refs, lam_init):
    k_refs = refs[:DIFF_PAGES]
    v_refs = refs[DIFF_PAGES:2 * DIFF_PAGES]
    o_ref, m_ref, l_ref, acc_ref = refs[2 * DIFF_PAGES:]
    g = pl.program_id(1)
    heads = q_ref.shape[1]
    rows = 16
    same_half = (lax.broadcasted_iota(I32, (rows, LANES), 0) // 8
                 == lax.broadcasted_iota(I32, (rows, LANES), 1) // HEAD_DIM)

    def stacked_q(h):
        return jnp.where(same_half, jnp.broadcast_to(q_ref[0, h] * SCALE, (rows, LANES)), 0.0)

    @pl.when(g == 0)
    def _():
        for h in range(heads):
            s0 = jnp.sum(stacked_q(h) * kn_ref[0, h], axis=-1, keepdims=True)
            m_ref[h] = jnp.broadcast_to(s0, (rows, LANES))
            l_ref[h] = jnp.ones((rows, LANES), F32)
            acc_ref[h] = jnp.broadcast_to(vn_ref[0, h], (rows, LANES))

    def score_head(h):
        qs = stacked_q(h).astype(BF16)
        return [lax.dot_general(qs, k_refs[c][0, 0, h].astype(BF16), _NT, preferred_element_type=F32)
                for c in range(DIFF_PAGES)]

    old = [(m_ref[h], l_ref[h], acc_ref[h]) for h in range(heads)]
    new = []
    upcoming = score_head(0)
    for h in range(heads):
        scores = upcoming
        if h + 1 < heads:
            upcoming = score_head(h + 1)
        top = scores[0]
        for s in scores[1:]:
            top = jnp.maximum(top, s)
        m_old, l, acc = old[h]
        m_new = jnp.maximum(m_old, jnp.max(top, axis=-1, keepdims=True))
        corr = jnp.exp(m_old - m_new)
        es = [jnp.exp(s - m_new) for s in scores]
        esum = es[0]
        for e in es[1:]:
            esum = esum + e
        acc = acc * corr
        for c in range(DIFF_PAGES):
            acc = acc + jnp.dot(es[c].astype(BF16), v_refs[c][0, 0, h].astype(BF16),
                                preferred_element_type=F32)
        new.append((m_new, l * corr + jnp.sum(esum, axis=-1, keepdims=True), acc))
    for h, (m_new, l, acc) in enumerate(new):
        m_ref[h] = m_new
        l_ref[h] = l
        acc_ref[h] = acc

    @pl.when(g == pl.num_programs(1) - 1)
    def _():
        lam = _lambda_value(lq1_ref, lk1_ref, lq2_ref, lk2_ref, lam_init)
        for h in range(heads):
            o = acc_ref[h] / l_ref[h]
            od = o[0:1] - lam * o[8:9]
            ms = jnp.mean(od * od, axis=-1, keepdims=True)
            o_ref[0, h] = od * lax.rsqrt(ms + RMS_EPS) * gsub_ref[...] * (1.0 - lam_init)


def _diff_rows(pt_flat, q4, kn4, vn4, lams, gsub_row, cache_k, cache_v, n_pages, lam_init):
    nbatch, heads = q4.shape[:2]
    tokv = lambda: pl.BlockSpec((1, heads, 1, LANES), lambda b, g, pt: (b, 0, 0, 0))
    vec = lambda: pl.BlockSpec((1, HEAD_DIM), lambda b, g, pt: (0, 0))
    page_block = (1, 1, heads, PAGE_SIZE, LANES)

    def page_spec(c):
        return pl.BlockSpec(page_block, lambda b, g, pt: (pt[b * n_pages + g * DIFF_PAGES + c], 0, 0, 0, 0))

    grid_spec = pltpu.PrefetchScalarGridSpec(
        num_scalar_prefetch=1,
        grid=(nbatch, n_pages // DIFF_PAGES),
        in_specs=[tokv(), tokv(), tokv(), vec(), vec(), vec(), vec(),
                  pl.BlockSpec((1, LANES), lambda b, g, pt: (0, 0))]
        + [page_spec(c) for c in range(DIFF_PAGES)] * 2,
        out_specs=tokv(),
        scratch_shapes=[pltpu.VMEM((heads, 16, LANES), F32)] * 3,
    )
    return pl.pallas_call(
        functools.partial(_diff_rows_kernel, lam_init=lam_init),
        grid_spec=grid_spec,
        out_shape=jax.ShapeDtypeStruct(q4.shape, F32),
        compiler_params=_cparams("parallel", "arbitrary"),
        name="diff_sample_rows",
    )(pt_flat, q4, kn4, vn4, *lams, gsub_row, *([cache_k] * DIFF_PAGES), *([cache_v] * DIFF_PAGES))


def _outproj_router_kernel(x_ref, ma_ref, md_ref, wo_ref, g_ref, wr_ref, br_ref,
                           xres_ref, xn_ref, eidx_ref, gate_ref, rank_ref, cnt_ref, carry_ref):
    i = pl.program_id(0)

    @pl.when(i == 0)
    def _():
        carry_ref[...] = jnp.zeros(carry_ref.shape, F32)

    mix = jnp.concatenate([ma_ref[...], md_ref[...]], axis=1).astype(BF16)
    xr = x_ref[...] + jnp.dot(mix, wo_ref[...], preferred_element_type=F32)
    xres_ref[...] = xr
    xn = _norm_rows(xr, g_ref[...])
    xn_ref[...] = xn

    tm = xn.shape[0]
    ne = wr_ref.shape[0]
    lt = lax.dot_general(wr_ref[...], xn, _NT, precision=_HI, preferred_element_type=F32) + br_ref[...]
    eio = lax.broadcasted_iota(I32, lt.shape, 0)
    vals, idxs, sels = [], [], []
    for _ in range(TOP_K):
        m = jnp.max(lt, axis=0, keepdims=True)
        idx = jnp.min(jnp.where(lt == m, eio, ne), axis=0, keepdims=True)
        sel = eio == idx
        vals.append(m)
        idxs.append(idx)
        sels.append(sel)
        lt = jnp.where(sel, -jnp.inf, lt)
    es = [jnp.exp(v - vals[0]) for v in vals]
    tot = es[0] + es[1] + es[2] + es[3]
    gate_ref[...] = jnp.concatenate([e / tot for e in es], axis=0)
    eidx_ref[...] = jnp.concatenate(idxs, axis=0)

    onehot = jnp.zeros(lt.shape, F32)
    for sel in sels:
        onehot = onehot + sel.astype(F32)
    before = (lax.broadcasted_iota(I32, (tm, tm), 0) < lax.broadcasted_iota(I32, (tm, tm), 1)).astype(BF16)
    prefix = jnp.dot(onehot.astype(BF16), before, preferred_element_type=F32) + carry_ref[:, 0:1]
    ranks = [jnp.sum(jnp.where(sel, prefix, 0.0), axis=0, keepdims=True) for sel in sels]
    rank_ref[...] = jnp.concatenate(ranks, axis=0).astype(I32)
    carry_ref[...] = carry_ref[...] + jnp.sum(onehot, axis=1, keepdims=True)
    cnt_ref[...] = carry_ref[...].astype(I32)


def _outproj_router(x2, mix_a, mix_d, wo_bf, g_ffn, wr_t, br_col, tm):
    t, d = x2.shape
    ne = wr_t.shape[0]
    row = lambda w: pl.BlockSpec((tm, w), lambda i: (i, 0))
    full = lambda shp: pl.BlockSpec(shp, lambda i: (0, 0))
    small = lambda: pl.BlockSpec((TOP_K, tm), lambda i: (0, i))
    return pl.pallas_call(
        _outproj_router_kernel,
        grid=(t // tm,),
        in_specs=[row(d), row(WIDTH), row(WIDTH), full(wo_bf.shape), full((1, d)), full((ne, d)), full((ne, 1))],
        out_specs=[row(d), row(d), small(), small(), small(), full((ne, LANES))],
        out_shape=[
            jax.ShapeDtypeStruct((t, d), F32), jax.ShapeDtypeStruct((t, d), F32),
            jax.ShapeDtypeStruct((TOP_K, t), I32), jax.ShapeDtypeStruct((TOP_K, t), F32),
            jax.ShapeDtypeStruct((TOP_K, t), I32), jax.ShapeDtypeStruct((ne, LANES), I32),
        ],
        scratch_shapes=[pltpu.VMEM((ne, LANES), F32)],
        compiler_params=_cparams("arbitrary"),
        name="outproj_router",
    )(x2, mix_a, mix_d, wo_bf, g_ffn, wr_t, br_col)


def _dispatch_kernel(dest_ref, xn_ref, buf_in_ref, buf_ref, sem):
    del buf_in_ref
    i = pl.program_id(0)
    tm = xn_ref.shape[0]

    def row_copy(r, k):
        d = dest_ref[(i * tm + r) * TOP_K + k]
        return pltpu.make_async_copy(xn_ref.at[pl.ds(r, 1), :], buf_ref.at[pl.ds(d, 1), :], sem)

    def start(r, carry):
        for k in range(TOP_K):
            row_copy(r, k).start()
        return carry

    def wait(r, carry):
        for k in range(TOP_K):
            row_copy(r, k).wait()
        return carry

    lax.fori_loop(0, tm, start, 0)
    lax.fori_loop(0, tm, wait, 0)


def _dispatch(dest_flat, xn, buf, tm):
    t, d = xn.shape
    grid_spec = pltpu.PrefetchScalarGridSpec(
        num_scalar_prefetch=1,
        grid=(t // tm,),
        in_specs=[pl.BlockSpec((tm, d), lambda i, dest: (i, 0)), pl.BlockSpec(memory_space=pl.ANY)],
        out_specs=pl.BlockSpec(memory_space=pl.ANY),
        scratch_shapes=[pltpu.SemaphoreType.DMA(())],
    )
    return pl.pallas_call(
        _dispatch_kernel,
        grid_spec=grid_spec,
        out_shape=jax.ShapeDtypeStruct(buf.shape, buf.dtype),
        input_output_aliases={2: 0},
        compiler_params=_cparams("arbitrary"),
        name="moe_dispatch",
    )(dest_flat, xn, buf)


def _experts_kernel(be_ref, nu_ref, x_ref, wg_ref, bg_ref, wu_ref, bu_ref, wd_ref, bd_ref, y_ref,
                    wgb_ref, wub_ref, wdb_ref):
    i = pl.program_id(0)
    active = i < nu_ref[0]
    new_expert = (i == 0) | (be_ref[i] != be_ref[jnp.maximum(i - 1, 0)])

    @pl.when(active & new_expert)
    def _():
        wgb_ref[...] = wg_ref[0].astype(BF16)
        wub_ref[...] = wu_ref[0].astype(BF16)
        wdb_ref[...] = wd_ref[0].astype(BF16)

    @pl.when(active)
    def _():
        x = x_ref[...].astype(BF16)
        g = jnp.dot(x, wgb_ref[...], preferred_element_type=F32) + bg_ref[0]
        u = jnp.dot(x, wub_ref[...], preferred_element_type=F32) + bu_ref[0]
        g = jnp.minimum(g, SWIGLU_LIMIT)
        u = jnp.clip(u, -SWIGLU_LIMIT, SWIGLU_LIMIT)
        h = (g * jax.nn.sigmoid(SWIGLU_ALPHA * g) * (u + 1.0)).astype(BF16)
        y_ref[...] = jnp.dot(h, wdb_ref[...], preferred_element_type=F32) + bd_ref[0]

    @pl.when(i >= nu_ref[0])
    def _():
        y_ref[...] = jnp.zeros(y_ref.shape, F32)


def _experts(block_e, n_used, buf, wg, bg, wu, bu, wd, bd):
    n_rows, d = buf.shape
    dff = wg.shape[2]
    wspec = lambda shp: pl.BlockSpec((1,) + shp, lambda i, be, nu: (be[i], 0, 0))
    grid_spec = pltpu.PrefetchScalarGridSpec(
        num_scalar_prefetch=2,
        grid=(n_rows // MOE_ROWS,),
        in_specs=[pl.BlockSpec((MOE_ROWS, d), lambda i, be, nu: (i, 0)),
                  wspec((d, dff)), wspec((1, dff)), wspec((d, dff)), wspec((1, dff)),
                  wspec((dff, d)), wspec((1, d))],
        out_specs=pl.BlockSpec((MOE_ROWS, d), lambda i, be, nu: (i, 0)),
        scratch_shapes=[pltpu.VMEM((d, dff), BF16), pltpu.VMEM((d, dff), BF16), pltpu.VMEM((dff, d), BF16)],
    )
    return pl.pallas_call(
        _experts_kernel,
        grid_spec=grid_spec,
        out_shape=jax.ShapeDtypeStruct((n_rows, d), F32),
        compiler_params=_cparams("arbitrary"),
        name="moe_experts",
    )(block_e, n_used, buf, wg, bg, wu, bu, wd, bd)


def _combine_kernel(dest_ref, xres_ref, gate_ref, gfin_ref, yb_ref, out_ref, rows_ref, sem):
    i = pl.program_id(0)
    tm = xres_ref.shape[0]

    def row_copy(r, k):
        d = dest_ref[(i * tm + r) * TOP_K + k]
        return pltpu.make_async_copy(yb_ref.at[pl.ds(d, 1), :], rows_ref.at[k, pl.ds(r, 1), :], sem)

    def start(r, carry):
        for k in range(TOP_K):
            row_copy(r, k).start()
        return carry

    def wait(r, carry):
        for k in range(TOP_K):
            row_copy(r, k).wait()
        return carry

    lax.fori_loop(0, tm, start, 0)
    lax.fori_loop(0, tm, wait, 0)
    gates = gate_ref[...]
    y = gates[:, 0:1] * rows_ref[0]
    for k in range(1, TOP_K):
        y = y + gates[:, k:k + 1] * rows_ref[k]
    out_ref[...] = _norm_rows(xres_ref[...] + y, gfin_ref[...])


def _combine(dest_flat, xres, gates_col, g_final, yb, tm):
    t, d = xres.shape
    grid_spec = pltpu.PrefetchScalarGridSpec(
        num_scalar_prefetch=1,
        grid=(t // tm,),
        in_specs=[pl.BlockSpec((tm, d), lambda i, dest: (i, 0)),
                  pl.BlockSpec((tm, TOP_K), lambda i, dest: (i, 0)),
                  pl.BlockSpec((1, d), lambda i, dest: (0, 0)),
                  pl.BlockSpec(memory_space=pl.ANY)],
        out_specs=pl.BlockSpec((tm, d), lambda i, dest: (i, 0)),
        scratch_shapes=[pltpu.VMEM((TOP_K, tm, d), F32), pltpu.SemaphoreType.DMA(())],
    )
    return pl.pallas_call(
        _combine_kernel,
        grid_spec=grid_spec,
        out_shape=jax.ShapeDtypeStruct((t, d), F32),
        compiler_params=_cparams("arbitrary"),
        name="moe_combine",
    )(dest_flat, xres, gates_col, g_final, yb)


def kernel(x_prompt, x_sample, cache_k_moba, cache_v_moba, cache_k_diff, cache_v_diff, page_table,
           g_mix, w_in, lambda_q1, lambda_k1, lambda_q2, lambda_k2, g_subln, w_out, g_ffn,
           w_router, b_router, w_gate, b_gate, w_up, b_up, w_down, b_down, g_final):
    depth = w_in.shape[0]
    assert depth == 1, "single-layer step"
    batch, seq, d_model = x_prompt.shape
    dec_batch, dec_seq, _ = x_sample.shape
    assert dec_seq == 1
    n_pages = page_table.shape[1]
    n_experts = w_router.shape[2]
    layer = 0
    lam_init = 0.8 - 0.6 * math.exp(-0.3 * layer)
    ppb = MOBA_BLOCK // PAGE_SIZE
    assert n_pages % ppb == 0 and n_pages // ppb >= MOBA_TOPK

    w_in_bf = w_in[layer].astype(BF16)
    w_out_bf = w_out[layer].astype(BF16)
    g_mix_row = g_mix[layer][None]
    lams = (lambda_q1[layer][None], lambda_k1[layer][None], lambda_q2[layer][None], lambda_k2[layer][None])
    pt_flat = page_table.reshape(-1)

    tabs_p = _rope_tables(jnp.arange(seq, dtype=I32))
    qa, ka, va, qd, ka_hm, va_hm, kd_hm, vd_hm = _inproj_prompt(x_prompt, g_mix_row, w_in_bf, tabs_p, tm=256)
    mix_a_p = _moba_prompt(qa, ka, va)
    mix_d_p = _diff_prompt(qd, kd_hm, vd_hm, lams, g_subln[layer][:, None], lam_init)

    pos_s = n_pages * PAGE_SIZE + jnp.zeros((dec_batch,), I32)
    tabs_s = _rope_tables(pos_s)
    p_s = _inproj_sample(x_sample.reshape(1, dec_batch, d_model), g_mix_row, w_in_bf, tabs_s)[0]
    qa_s, ka_s, va_s, qd_s, kd_s, vd_s = (p_s[:, s * WIDTH:(s + 1) * WIDTH] for s in range(6))
    heads4 = lambda a, h: a.reshape(dec_batch, h, 1, WIDTH // h)
    cols4 = lambda a, h: a.reshape(dec_batch, h, WIDTH // h, 1)
    cache_kt_moba = jnp.swapaxes(cache_k_moba, 3, 4)
    cache_vt_moba = jnp.swapaxes(cache_v_moba, 3, 4)
    sel = _moba_gate(pt_flat, cols4(qa_s, N_HEADS_MOBA), cache_kt_moba, n_pages)
    sel = sel[:, :, :MOBA_TOPK]
    logical = sel[..., None] * ppb + jnp.arange(ppb, dtype=I32)
    phys = jnp.take_along_axis(page_table, logical.reshape(dec_batch, -1), axis=1)
    mix_a_s = _moba_sample_attn(phys.reshape(-1), heads4(qa_s, N_HEADS_MOBA), heads4(ka_s, N_HEADS_MOBA),
                                heads4(va_s, N_HEADS_MOBA), cache_kt_moba, cache_vt_moba, MOBA_TOPK * ppb,
                                layer)
    mix_d_s = _diff_rows(pt_flat, heads4(qd_s, N_HEADS_DIFF), heads4(kd_s, N_HEADS_DIFF),
                         heads4(vd_s, N_HEADS_DIFF), lams, g_subln[layer][None],
                         cache_k_diff, cache_v_diff, n_pages, lam_init)

    wr_t = w_router[layer].T
    br_col = b_router[layer][:, None]
    g_ffn_row = g_ffn[layer][None]
    t_p = batch * seq
    xres_p, xn_p, eidx_p, gate_p, rank_p, cnt_p = _outproj_router(
        x_prompt.reshape(t_p, d_model), mix_a_p.reshape(t_p, WIDTH), mix_d_p.reshape(t_p, WIDTH),
        w_out_bf, g_ffn_row, wr_t, br_col, tm=512)
    xres_s, xn_s, eidx_s, gate_s, rank_s, cnt_s = _outproj_router(
        x_sample.reshape(dec_batch, d_model), mix_a_s.reshape(dec_batch, WIDTH),
        mix_d_s.reshape(dec_batch, WIDTH), w_out_bf, g_ffn_row, wr_t, br_col, tm=dec_batch)

    cnt_p = cnt_p[:, 0]
    sizes = cnt_p + cnt_s[:, 0]
    padded = (sizes + MOE_ROWS - 1) // MOE_ROWS * MOE_ROWS
    pends = jnp.cumsum(padded)
    pstart = pends - padded
    n_assign = (t_p + dec_batch) * TOP_K
    n_blocks = -(-(n_assign + n_experts * (MOE_ROWS - 1)) // MOE_ROWS)
    eids = jnp.arange(n_experts, dtype=I32)
    lookup = lambda table, eidx: jnp.sum(jnp.where(eidx[..., None] == eids, table, 0), axis=-1)
    dest_p = (lookup(pstart, eidx_p) + rank_p).T.reshape(-1)
    dest_s = (lookup(pstart + cnt_p, eidx_s) + rank_s).T.reshape(-1)
    block_starts = jnp.arange(n_blocks, dtype=I32) * MOE_ROWS
    block_e = jnp.minimum(jnp.sum((pends[None, :] <= block_starts[:, None]).astype(I32), axis=1),
                          n_experts - 1)
    n_used = (pends[-1:] // MOE_ROWS).astype(I32)

    buf = jnp.zeros((n_blocks * MOE_ROWS, d_model), F32)
    buf = _dispatch(dest_p, xn_p, buf, tm=512)
    buf = _dispatch(dest_s, xn_s, buf, tm=dec_batch)
    yb = _experts(block_e, n_used, buf,
                  w_gate[layer], b_gate[layer][:, None, :],
                  w_up[layer], b_up[layer][:, None, :],
                  w_down[layer], b_down[layer][:, None, :])
    g_fin_row = g_final[None]
    y_p = _combine(dest_p, xres_p, gate_p.T, g_fin_row, yb, tm=256)
    y_s = _combine(dest_s, xres_s, gate_s.T, g_fin_row, yb, tm=dec_batch)

    y_prompt = y_p.reshape(batch, seq, d_model)
    y_sample = y_s.reshape(dec_batch, 1, d_model)
    k_moba_sample = ka_s.reshape(dec_batch, 1, N_HEADS_MOBA, 1, HEAD_DIM)
    v_moba_sample = va_s.reshape(dec_batch, 1, N_HEADS_MOBA, 1, HEAD_DIM)
    k_diff_sample = kd_s.reshape(dec_batch, 1, N_HEADS_DIFF, 1, LANES)
    v_diff_sample = vd_s.reshape(dec_batch, 1, N_HEADS_DIFF, 1, LANES)
    return (y_prompt, y_sample, ka_hm, va_hm, kd_hm, vd_hm,
            k_moba_sample, v_moba_sample, k_diff_sample, v_diff_sample)
```

```python
import functools
import math

import jax
import jax.numpy as jnp
from jax import lax
from jax.experimental import pallas as pl
from jax.experimental.pallas import tpu as pltpu

F32 = jnp.float32
BF16 = jnp.bfloat16
I32 = jnp.int32

HEAD_DIM = 64
N_HEADS_MOBA = 8
N_HEADS_DIFF = 4
WIDTH = 512
ROPE_THETA = 500000.0
ROT_DIM = HEAD_DIM // 4
MOBA_BLOCK = 256
MOBA_TOPK = 3
PAGE_SIZE = 128
TOP_K = 4
SWIGLU_LIMIT = 7.0
SWIGLU_ALPHA = 1.702
RMS_EPS = 1e-5
NEG_INF = -1e30
SCALE = HEAD_DIM ** -0.5

LANES = 128
MOE_ROWS = 256
VMEM_LIMIT = 48 * 1024 * 1024

_NT = (((1,), (1,)), ((), ()))
_HI = lax.Precision.HIGHEST


def _cparams(*sem):
    return pltpu.CompilerParams(dimension_semantics=sem, vmem_limit_bytes=VMEM_LIMIT)


def _rope_tables(pos):
    half = ROT_DIM // 2
    inv = ROPE_THETA ** (-jnp.arange(half, dtype=F32) * (2.0 / ROT_DIM))
    ang = pos.astype(F32)[:, None] * inv[None, :]
    cos = jnp.cos(ang)
    sin = jnp.sin(ang)
    s = pos.shape[0]
    pad = HEAD_DIM - ROT_DIM
    c64 = jnp.concatenate([cos, cos, jnp.ones((s, pad), F32)], axis=1)
    a64 = jnp.concatenate([-sin, jnp.zeros((s, half + pad), F32)], axis=1)
    b64 = jnp.concatenate([jnp.zeros((s, half), F32), sin, jnp.zeros((s, pad), F32)], axis=1)
    rep = LANES // HEAD_DIM
    return jnp.tile(c64, (1, rep)), jnp.tile(a64, (1, rep)), jnp.tile(b64, (1, rep))


def _norm_rows(x, g):
    ms = jnp.mean(x * x, axis=-1, keepdims=True)
    return x * lax.rsqrt(ms + RMS_EPS) * g


def _rope_section(p, c, sa, sb):
    half = ROT_DIM // 2
    outs = []
    for j in range(p.shape[1] // LANES):
        xc = p[:, j * LANES:(j + 1) * LANES]
        outs.append(xc * c + pltpu.roll(xc, LANES - half, 1) * sa + pltpu.roll(xc, half, 1) * sb)
    return jnp.concatenate(outs, axis=1)


def _inproj_sections(x_ref, g_ref, w_ref, c_ref, sa_ref, sb_ref):
    xn = _norm_rows(x_ref[0], g_ref[...]).astype(BF16)
    c, sa, sb = c_ref[...], sa_ref[...], sb_ref[...]
    secs = []
    for s in range(6):
        p = jnp.dot(xn, w_ref[:, s * WIDTH:(s + 1) * WIDTH], preferred_element_type=F32)
        if s in (0, 1, 3, 4):
            p = _rope_section(p, c, sa, sb)
        secs.append(p)
    return secs


def _inproj_prompt_kernel(x_ref, g_ref, w_ref, c_ref, sa_ref, sb_ref,
                          qa_ref, ka_ref, va_ref, qd_ref, kahm_ref, vahm_ref, kdhm_ref, vdhm_ref):
    qa, ka, va, qd, kd, vd = _inproj_sections(x_ref, g_ref, w_ref, c_ref, sa_ref, sb_ref)
    qa_ref[0] = qa
    ka_ref[0] = ka
    va_ref[0] = va
    qd_ref[0] = qd
    for h in range(N_HEADS_MOBA):
        kahm_ref[0, 0, h] = ka[:, h * HEAD_DIM:(h + 1) * HEAD_DIM]
        vahm_ref[0, 0, h] = va[:, h * HEAD_DIM:(h + 1) * HEAD_DIM]
    for h in range(N_HEADS_DIFF):
        kdhm_ref[0, 0, h] = kd[:, h * LANES:(h + 1) * LANES]
        vdhm_ref[0, 0, h] = vd[:, h * LANES:(h + 1) * LANES]


def _inproj_sample_kernel(x_ref, g_ref, w_ref, c_ref, sa_ref, sb_ref, p_ref):
    secs = _inproj_sections(x_ref, g_ref, w_ref, c_ref, sa_ref, sb_ref)
    for s in range(6):
        p_ref[0, :, s * WIDTH:(s + 1) * WIDTH] = secs[s]


def _inproj_in_specs(tm, d, pw):
    return [
        pl.BlockSpec((1, tm, d), lambda b, i: (b, i, 0)),
        pl.BlockSpec((1, d), lambda b, i: (0, 0)),
        pl.BlockSpec((d, pw), lambda b, i: (0, 0)),
        pl.BlockSpec((tm, LANES), lambda b, i: (i, 0)),
        pl.BlockSpec((tm, LANES), lambda b, i: (i, 0)),
        pl.BlockSpec((tm, LANES), lambda b, i: (i, 0)),
    ]


def _inproj_prompt(x, g, w_bf, tabs, tm):
    b, s, d = x.shape
    pw = w_bf.shape[1]
    tok = lambda: pl.BlockSpec((1, tm, WIDTH), lambda bb, i: (bb, i, 0))
    hm_a = lambda: pl.BlockSpec((1, 1, N_HEADS_MOBA, tm, HEAD_DIM), lambda bb, i: (bb, 0, 0, i, 0))
    hm_d = lambda: pl.BlockSpec((1, 1, N_HEADS_DIFF, tm, LANES), lambda bb, i: (bb, 0, 0, i, 0))
    tok_shape = jax.ShapeDtypeStruct((b, s, WIDTH), F32)
    hma_shape = jax.ShapeDtypeStruct((b, 1, N_HEADS_MOBA, s, HEAD_DIM), F32)
    hmd_shape = jax.ShapeDtypeStruct((b, 1, N_HEADS_DIFF, s, LANES), F32)
    return pl.pallas_call(
        _inproj_prompt_kernel,
        grid=(b, s // tm),
        in_specs=_inproj_in_specs(tm, d, pw),
        out_specs=[tok(), tok(), tok(), tok(), hm_a(), hm_a(), hm_d(), hm_d()],
        out_shape=[tok_shape, tok_shape, tok_shape, tok_shape, hma_shape, hma_shape, hmd_shape, hmd_shape],
        compiler_params=_cparams("parallel", "parallel"),
        name="inproj_prompt",
    )(x, g, w_bf, *tabs)


def _inproj_sample(x, g, w_bf, tabs):
    b, s, d = x.shape
    pw = w_bf.shape[1]
    return pl.pallas_call(
        _inproj_sample_kernel,
        grid=(b, 1),
        in_specs=_inproj_in_specs(s, d, pw),
        out_specs=pl.BlockSpec((1, s, pw), lambda bb, i: (bb, 0, 0)),
        out_shape=jax.ShapeDtypeStruct((b, s, pw), F32),
        compiler_params=_cparams("parallel", "parallel"),
        name="inproj_sample",
    )(x, g, w_bf, *tabs)


def _stage_kv(k_rows, v_rows, kb_ref, vt_ref, nb):
    for n in range(nb):
        sl = slice(n * MOBA_BLOCK, (n + 1) * MOBA_BLOCK)
        kb_ref[n] = k_rows(sl).astype(BF16)
        vt_ref[n] = v_rows(sl).T.astype(BF16)


def _own_block(kb, vt, qs):
    st = lax.dot_general(kb, qs, _NT, preferred_element_type=F32)
    kpos = lax.broadcasted_iota(I32, st.shape, 0)
    qpos = lax.broadcasted_iota(I32, st.shape, 1)
    st = jnp.where(kpos <= qpos, st, NEG_INF)
    m = jnp.max(st, axis=0, keepdims=True)
    e = jnp.exp(st - m)
    l = jnp.sum(e, axis=0, keepdims=True)
    acc = jnp.dot(vt, e.astype(BF16), preferred_element_type=F32)
    return m, l, acc


def _past_block(carry, kb, vt, qs, keep):
    m, l, acc = carry
    st = lax.dot_general(kb, qs, _NT, preferred_element_type=F32)
    if keep is not None:
        st = jnp.where(keep, st, NEG_INF)
    m_new = jnp.maximum(m, jnp.max(st, axis=0, keepdims=True))
    c = jnp.exp(m - m_new)
    e = jnp.exp(st - m_new)
    l = l * c + jnp.sum(e, axis=0, keepdims=True)
    acc = acc * c + jnp.dot(vt, e.astype(BF16), preferred_element_type=F32)
    return m_new, l, acc


def _moba_prompt_kernel(q_ref, k_ref, v_ref, o_ref, kb_ref, vt_ref, kmean_ref, sel_ref, *, nb):
    i = pl.program_id(2)

    @pl.when(i == 0)
    def _():
        _stage_kv(lambda sl: k_ref[0, sl, :], lambda sl: v_ref[0, sl, :], kb_ref, vt_ref, nb)
        for n in range(nb):
            kmean_ref[n:n + 1, :] = jnp.mean(k_ref[0, n * MOBA_BLOCK:(n + 1) * MOBA_BLOCK, :],
                                             axis=0, keepdims=True)

    q = q_ref[0]
    lane = lax.broadcasted_iota(I32, q.shape, 1)
    blk = lax.broadcasted_iota(I32, (nb, q.shape[0]), 0)
    n_heads = LANES // HEAD_DIM
    vsl = [slice(h * HEAD_DIM, (h + 1) * HEAD_DIM) for h in range(n_heads)]
    qs = []
    for h in range(n_heads):
        qh = jnp.where(lane // HEAD_DIM == h, q, 0.0)
        gt = lax.dot_general(kmean_ref[...], qh, _NT, precision=_HI, preferred_element_type=F32)
        gm = jnp.where(blk < i, gt, NEG_INF)
        beaten = jnp.zeros(gm.shape, I32)
        for jp in range(nb):
            row = gm[jp:jp + 1, :]
            beaten = beaten + ((row > gm) | ((row == gm) & (jp < blk))).astype(I32)
        sel_ref[h] = ((beaten < MOBA_TOPK) & (blk < i)).astype(F32)
        qs.append((qh * SCALE).astype(BF16))

    kb_own = kb_ref[i]
    carry = tuple(_own_block(kb_own, vt_ref[i, vsl[h], :], qs[h]) for h in range(n_heads))

    def body(j, carry):
        kb = kb_ref[j]
        return tuple(_past_block(carry[h], kb, vt_ref[j, vsl[h], :], qs[h],
                                 sel_ref[h, pl.ds(j, 1), :] > 0.0) for h in range(n_heads))

    carry = lax.fori_loop(0, i, body, carry)
    o_ref[0] = jnp.concatenate([acc / l for _, l, acc in carry], axis=0).T


def _moba_prompt(qa, ka, va):
    b, s, _ = qa.shape
    nb = s // MOBA_BLOCK
    tq = MOBA_BLOCK
    return pl.pallas_call(
        functools.partial(_moba_prompt_kernel, nb=nb),
        grid=(b, WIDTH // LANES, nb),
        in_specs=[
            pl.BlockSpec((1, tq, LANES), lambda bb, hp, i: (bb, i, hp)),
            pl.BlockSpec((1, s, LANES), lambda bb, hp, i: (bb, 0, hp)),
            pl.BlockSpec((1, s, LANES), lambda bb, hp, i: (bb, 0, hp)),
        ],
        out_specs=pl.BlockSpec((1, tq, LANES), lambda bb, hp, i: (bb, i, hp)),
        out_shape=jax.ShapeDtypeStruct((b, s, WIDTH), F32),
        scratch_shapes=[
            pltpu.VMEM((nb, MOBA_BLOCK, LANES), BF16),
            pltpu.VMEM((nb, LANES, MOBA_BLOCK), BF16),
            pltpu.VMEM((nb, LANES), F32),
            pltpu.VMEM((LANES // HEAD_DIM, nb, tq), F32),
        ],
        compiler_params=_cparams("parallel", "parallel", "arbitrary"),
        name="moba_prompt_attn",
    )(qa, ka, va)


def _lambda_value(lq1_ref, lk1_ref, lq2_ref, lk2_ref, lam_init):
    a = jnp.sum(lq1_ref[...] * lk1_ref[...], axis=-1, keepdims=True)
    c = jnp.sum(lq2_ref[...] * lk2_ref[...], axis=-1, keepdims=True)
    return jnp.exp(a) - jnp.exp(c) + lam_init


def _diff_prompt_kernel(q_ref, k_ref, v_ref, lq1_ref, lk1_ref, lq2_ref, lk2_ref, gsub_ref,
                        o_ref, kb_ref, vt_ref, *, nb, lam_init):
    i = pl.program_id(2)

    @pl.when(i == 0)
    def _():
        _stage_kv(lambda sl: k_ref[0, 0, 0, sl, :], lambda sl: v_ref[0, 0, 0, sl, :], kb_ref, vt_ref, nb)

    lam = _lambda_value(lq1_ref, lk1_ref, lq2_ref, lk2_ref, lam_init)
    q = q_ref[0]
    lane = lax.broadcasted_iota(I32, q.shape, 1)
    qs = [(jnp.where(lane // HEAD_DIM == mp, q, 0.0) * SCALE).astype(BF16) for mp in range(2)]
    kb_own, vt_own = kb_ref[i], vt_ref[i]
    carry = tuple(_own_block(kb_own, vt_own, qs[mp]) for mp in range(2))

    def body(j, carry):
        kb, vt = kb_ref[j], vt_ref[j]
        return tuple(_past_block(carry[mp], kb, vt, qs[mp], None) for mp in range(2))

    carry = lax.fori_loop(0, i, body, carry)
    maps = [acc / l for _, l, acc in carry]
    ot = maps[0] - lam * maps[1]
    ms = jnp.mean(ot * ot, axis=0, keepdims=True)
    y = ot * lax.rsqrt(ms + RMS_EPS) * gsub_ref[...] * (1.0 - lam_init)
    o_ref[0] = y.T


def _diff_prompt(qd, kd_hm, vd_hm, lams, gsub_col, lam_init):
    b, s, _ = qd.shape
    nb = s // MOBA_BLOCK
    tq = MOBA_BLOCK
    vec = lambda: pl.BlockSpec((1, HEAD_DIM), lambda bb, h, i: (0, 0))
    return pl.pallas_call(
        functools.partial(_diff_prompt_kernel, nb=nb, lam_init=lam_init),
        grid=(b, N_HEADS_DIFF, nb),
        in_specs=[
            pl.BlockSpec((1, tq, LANES), lambda bb, h, i: (bb, i, h)),
            pl.BlockSpec((1, 1, 1, s, LANES), lambda bb, h, i: (bb, 0, h, 0, 0)),
            pl.BlockSpec((1, 1, 1, s, LANES), lambda bb, h, i: (bb, 0, h, 0, 0)),
            vec(), vec(), vec(), vec(),
            pl.BlockSpec((LANES, 1), lambda bb, h, i: (0, 0)),
        ],
        out_specs=pl.BlockSpec((1, tq, LANES), lambda bb, h, i: (bb, i, h)),
        out_shape=jax.ShapeDtypeStruct((b, s, WIDTH), F32),
        scratch_shapes=[
            pltpu.VMEM((nb, MOBA_BLOCK, LANES), BF16),
            pltpu.VMEM((nb, LANES, MOBA_BLOCK), BF16),
        ],
        compiler_params=_cparams("parallel", "parallel", "arbitrary"),
        name="diff_prompt_attn",
    )(qd, kd_hm, vd_hm, *lams, gsub_col)


MOBA_GATE_PAGES = 16
DIFF_PAGES = 8


def _moba_gate_kernel(pt_ref, q_ref, *refs, n_pages):
    k_refs = refs[:MOBA_GATE_PAGES]
    sel_ref = refs[MOBA_GATE_PAGES]
    part_ref = refs[MOBA_GATE_PAGES + 1]
    g = pl.program_id(1)
    ppb = MOBA_BLOCK // PAGE_SIZE
    heads, hd = q_ref.shape[1:3]
    qb = jnp.broadcast_to(q_ref[0], (heads, hd, PAGE_SIZE))
    for n in range(MOBA_GATE_PAGES // ppb):
        ksum = k_refs[n * ppb][0, 0]
        for c in range(1, ppb):
            ksum = ksum + k_refs[n * ppb + c][0, 0]
        part_ref[g * (MOBA_GATE_PAGES // ppb) + n] = jnp.sum(ksum * qb, axis=1)

    @pl.when(g == pl.num_programs(1) - 1)
    def _():
        nb = n_pages // ppb
        gate = jnp.sum(part_ref[...], axis=-1, keepdims=True) * (1.0 / MOBA_BLOCK)
        bidx = lax.broadcasted_iota(I32, gate.shape, 0)
        lane = lax.broadcasted_iota(I32, sel_ref.shape[1:], 1)
        out = jnp.zeros(sel_ref.shape[1:], I32)
        for r in range(MOBA_TOPK):
            m = jnp.max(gate, axis=0)
            idx = jnp.min(jnp.where(gate == m[None], bidx, nb), axis=0)
            out = jnp.where(lane == r, idx, out)
            gate = jnp.where(bidx == idx[None], -jnp.inf, gate)
        sel_ref[0] = out


def _moba_gate(pt_flat, q_cols, cache_kt, n_pages):
    nbatch, heads, hd, _ = q_cols.shape
    page_block = (1, 1, heads, hd, PAGE_SIZE)

    def page_spec(c):
        return pl.BlockSpec(page_block,
                            lambda b, g, pt: (pt[b * n_pages + g * MOBA_GATE_PAGES + c], 0, 0, 0, 0))

    grid_spec = pltpu.PrefetchScalarGridSpec(
        num_scalar_prefetch=1,
        grid=(nbatch, n_pages // MOBA_GATE_PAGES),
        in_specs=[pl.BlockSpec((1, heads, hd, 1), lambda b, g, pt: (b, 0, 0, 0))]
        + [page_spec(c) for c in range(MOBA_GATE_PAGES)],
        out_specs=pl.BlockSpec((1, heads, LANES), lambda b, g, pt: (b, 0, 0)),
        scratch_shapes=[pltpu.VMEM((n_pages * PAGE_SIZE // MOBA_BLOCK, heads, PAGE_SIZE), F32)],
    )
    return pl.pallas_call(
        functools.partial(_moba_gate_kernel, n_pages=n_pages),
        grid_spec=grid_spec,
        out_shape=jax.ShapeDtypeStruct((nbatch, heads, LANES), I32),
        compiler_params=_cparams("parallel", "arbitrary"),
        name="moba_sample_gate",
    )(pt_flat, q_cols, *([cache_kt] * MOBA_GATE_PAGES))


def _moba_sample_attn_kernel(ph_ref, q_ref, kn_ref, vn_ref, kt_hbm, vt_hbm, o_ref,
                             kbuf_ref, vbuf_ref, sem, *, n_sel_pages, layer):
    b = pl.program_id(0)
    heads = q_ref.shape[1]
    slot = b % 2

    def slab_copies(seq, slot):
        copies = []
        for h in range(heads):
            for c in range(n_sel_pages):
                j = h * n_sel_pages + c
                page = ph_ref[seq * (heads * n_sel_pages) + j]
                copies.append(pltpu.make_async_copy(kt_hbm.at[page, layer, h], kbuf_ref.at[slot, j], sem.at[slot]))
                copies.append(pltpu.make_async_copy(vt_hbm.at[page, layer, h], vbuf_ref.at[slot, j], sem.at[slot]))
        return copies

    @pl.when(b == 0)
    def _():
        for cp in slab_copies(b, slot):
            cp.start()

    @pl.when(b + 1 < pl.num_programs(0))
    def _():
        for cp in slab_copies(b + 1, 1 - slot):
            cp.start()

    for cp in slab_copies(b, slot):
        cp.wait()

    for h in range(heads):
        q = q_ref[0, h] * SCALE
        qs = jnp.broadcast_to(q, (8, HEAD_DIM)).astype(BF16)
        s_own = jnp.sum(q * kn_ref[0, h], axis=-1, keepdims=True)
        kts = [kbuf_ref[slot, h * n_sel_pages + c].astype(BF16) for c in range(n_sel_pages)]
        s_pages = [jnp.dot(qs, kt, preferred_element_type=F32) for kt in kts]
        m = s_own
        for s in s_pages:
            m = jnp.maximum(m, jnp.max(s, axis=-1, keepdims=True))
        e_own = jnp.exp(s_own - m)
        l = e_own
        acc = e_own * vn_ref[0, h]
        for c in range(n_sel_pages):
            e = jnp.exp(s_pages[c] - m)
            l = l + jnp.sum(e, axis=-1, keepdims=True)
            vt = vbuf_ref[slot, h * n_sel_pages + c].astype(BF16)
            acc = acc + lax.dot_general(e.astype(BF16), vt, _NT, preferred_element_type=F32)
        o_ref[0, h] = (acc / l)[0:1]


def _moba_sample_attn(phys_flat, q4, kn4, vn4, cache_kt, cache_vt, n_sel_pages, layer):
    nbatch, heads = q4.shape[:2]
    vec = lambda: pl.BlockSpec((1, heads, 1, HEAD_DIM), lambda b, ph: (b, 0, 0, 0))
    n_slabs = heads * n_sel_pages
    grid_spec = pltpu.PrefetchScalarGridSpec(
        num_scalar_prefetch=1,
        grid=(nbatch,),
        in_specs=[vec(), vec(), vec(), pl.BlockSpec(memory_space=pl.ANY), pl.BlockSpec(memory_space=pl.ANY)],
        out_specs=vec(),
        scratch_shapes=[pltpu.VMEM((2, n_slabs, HEAD_DIM, PAGE_SIZE), F32),
                        pltpu.VMEM((2, n_slabs, HEAD_DIM, PAGE_SIZE), F32),
                        pltpu.SemaphoreType.DMA((2,))],
    )
    return pl.pallas_call(
        functools.partial(_moba_sample_attn_kernel, n_sel_pages=n_sel_pages, layer=layer),
        grid_spec=grid_spec,
        out_shape=jax.ShapeDtypeStruct(q4.shape, F32),
        compiler_params=_cparams("arbitrary"),
        name="moba_sample_attn",
    )(phys_flat, q4, kn4, vn4, cache_kt, cache_vt)


def _diff_rows_kernel(pt_ref, q_ref, kn_ref, vn_ref, lq1_ref, lk1_ref, lq2_ref, lk2_ref, gsub_ref,
                      k_hbm, v_hbm, o_ref, m_ref, l_ref, acc_ref, kbuf_ref, vbuf_ref, sem,
                      *, lam_init, n_pages, layer):
    b = pl.program_id(0)
    g = pl.program_id(1)
    n_groups = pl.num_programs(1)
    step = b * n_groups + g
    slot = step % 2

    def page_copies(seq, group, slot):
        copies = []
        for c in range(DIFF_PAGES):
            page = pt_ref[seq * n_pages + group * DIFF_PAGES + c]
            copies.append(pltpu.make_async_copy(k_hbm.at[page, layer], kbuf_ref.at[slot, c], sem.at[slot]))
            copies.append(pltpu.make_async_copy(v_hbm.at[page, layer], vbuf_ref.at[slot, c], sem.at[slot]))
        return copies

    @pl.when(step == 0)
    def _():
        for cp in page_copies(b, g, slot):
            cp.start()

    @pl.when(step + 1 < pl.num_programs(0) * n_groups)
    def _():
        wrap = g + 1 == n_groups
        for cp in page_copies(jnp.where(wrap, b + 1, b), jnp.where(wrap, 0, g + 1), 1 - slot):
            cp.start()

    for cp in page_copies(b, g, slot):
        cp.wait()

    k_refs = [kbuf_ref.at[slot, c] for c in range(DIFF_PAGES)]
    v_refs = [vbuf_ref.at[slot, c] for c in range(DIFF_PAGES)]
    heads = q_ref.shape[1]
    rows = 16
    same_half = (lax.broadcasted_iota(I32, (rows, LANES), 0) // 8
                 == lax.broadcasted_iota(I32, (rows, LANES), 1) // HEAD_DIM)

    def stacked_q(h):
        return jnp.where(same_half, jnp.broadcast_to(q_ref[0, h] * SCALE, (rows, LANES)), 0.0)

    @pl.when(g == 0)
    def _():
        for h in range(heads):
            s0 = jnp.sum(stacked_q(h) * kn_ref[0, h], axis=-1, keepdims=True)
            m_ref[h] = jnp.broadcast_to(s0, (rows, LANES))
            l_ref[h] = jnp.ones((rows, LANES), F32)
            acc_ref[h] = jnp.broadcast_to(vn_ref[0, h], (rows, LANES))

    def score_head(h):
        qs = stacked_q(h).astype(BF16)
        return [lax.dot_general(qs, k_refs[c][h].astype(BF16), _NT, preferred_element_type=F32)
                for c in range(DIFF_PAGES)]

    old = [(m_ref[h], l_ref[h], acc_ref[h]) for h in range(heads)]
    new = []
    upcoming = score_head(0)
    for h in range(heads):
        scores = upcoming
        if h + 1 < heads:
            upcoming = score_head(h + 1)
        top = scores[0]
        for s in scores[1:]:
            top = jnp.maximum(top, s)
        m_old, l, acc = old[h]
        m_new = jnp.maximum(m_old, jnp.max(top, axis=-1, keepdims=True))
        corr = jnp.exp(m_old - m_new)
        es = [jnp.exp(s - m_new) for s in scores]
        esum = es[0]
        for e in es[1:]:
            esum = esum + e
        pv = [jnp.dot(es[c].astype(BF16), v_refs[c][h].astype(BF16), preferred_element_type=F32)
              for c in range(DIFF_PAGES)]
        even = pv[0]
        for p in pv[2::2]:
            even = even + p
        odd = pv[1]
        for p in pv[3::2]:
            odd = odd + p
        acc = acc * corr + (even + odd)
        new.append((m_new, l * corr + jnp.sum(esum, axis=-1, keepdims=True), acc))
    for h, (m_new, l, acc) in enumerate(new):
        m_ref[h] = m_new
        l_ref[h] = l
        acc_ref[h] = acc

    @pl.when(g == pl.num_programs(1) - 1)
    def _():
        lam = _lambda_value(lq1_ref, lk1_ref, lq2_ref, lk2_ref, lam_init)
        for h in range(heads):
            o = acc_ref[h] / l_ref[h]
            od = o[0:1] - lam * o[8:9]
            ms = jnp.mean(od * od, axis=-1, keepdims=True)
            o_ref[0, h] = od * lax.rsqrt(ms + RMS_EPS) * gsub_ref[...] * (1.0 - lam_init)


def _diff_rows(pt_flat, q4, kn4, vn4, lams, gsub_row, cache_k, cache_v, n_pages, lam_init, layer):
    nbatch, heads = q4.shape[:2]
    tokv = lambda: pl.BlockSpec((1, heads, 1, LANES), lambda b, g, pt: (b, 0, 0, 0))
    vec = lambda: pl.BlockSpec((1, HEAD_DIM), lambda b, g, pt: (0, 0))
    page_buf = pltpu.VMEM((2, DIFF_PAGES, heads, PAGE_SIZE, LANES), F32)
    grid_spec = pltpu.PrefetchScalarGridSpec(
        num_scalar_prefetch=1,
        grid=(nbatch, n_pages // DIFF_PAGES),
        in_specs=[tokv(), tokv(), tokv(), vec(), vec(), vec(), vec(),
                  pl.BlockSpec((1, LANES), lambda b, g, pt: (0, 0)),
                  pl.BlockSpec(memory_space=pl.ANY), pl.BlockSpec(memory_space=pl.ANY)],
        out_specs=tokv(),
        scratch_shapes=[pltpu.VMEM((heads, 16, LANES), F32)] * 3
        + [page_buf, page_buf, pltpu.SemaphoreType.DMA((2,))],
    )
    return pl.pallas_call(
        functools.partial(_diff_rows_kernel, lam_init=lam_init, n_pages=n_pages, layer=layer),
        grid_spec=grid_spec,
        out_shape=jax.ShapeDtypeStruct(q4.shape, F32),
        compiler_params=_cparams("arbitrary", "arbitrary"),
        name="diff_sample_rows",
    )(pt_flat, q4, kn4, vn4, *lams, gsub_row, cache_k, cache_v)


def _outproj_router_kernel(x_ref, ma_ref, md_ref, wo_ref, g_ref, wr_ref, br_ref,
                           xres_ref, xn_ref, eidx_ref, gate_ref, rank_ref, cnt_ref, carry_ref):
    i = pl.program_id(0)

    @pl.when(i == 0)
    def _():
        carry_ref[...] = jnp.zeros(carry_ref.shape, F32)

    mix = jnp.concatenate([ma_ref[...], md_ref[...]], axis=1).astype(BF16)
    xr = x_ref[...] + jnp.dot(mix, wo_ref[...], preferred_element_type=F32)
    xres_ref[...] = xr
    xn = _norm_rows(xr, g_ref[...])
    xn_ref[...] = xn

    tm = xn.shape[0]
    ne = wr_ref.shape[0]
    lt = lax.dot_general(wr_ref[...], xn, _NT, precision=_HI, preferred_element_type=F32) + br_ref[...]
    eio = lax.broadcasted_iota(I32, lt.shape, 0)
    vals, idxs, sels = [], [], []
    for _ in range(TOP_K):
        m = jnp.max(lt, axis=0, keepdims=True)
        idx = jnp.min(jnp.where(lt == m, eio, ne), axis=0, keepdims=True)
        sel = eio == idx
        vals.append(m)
        idxs.append(idx)
        sels.append(sel)
        lt = jnp.where(sel, -jnp.inf, lt)
    es = [jnp.exp(v - vals[0]) for v in vals]
    tot = es[0] + es[1] + es[2] + es[3]
    gate_ref[...] = jnp.concatenate([e / tot for e in es], axis=0)
    eidx_ref[...] = jnp.concatenate(idxs, axis=0)

    onehot = jnp.zeros(lt.shape, F32)
    for sel in sels:
        onehot = onehot + sel.astype(F32)
    before = (lax.broadcasted_iota(I32, (tm, tm), 0) < lax.broadcasted_iota(I32, (tm, tm), 1)).astype(BF16)
    prefix = jnp.dot(onehot.astype(BF16), before, preferred_element_type=F32) + carry_ref[:, 0:1]
    ranks = [jnp.sum(jnp.where(sel, prefix, 0.0), axis=0, keepdims=True) for sel in sels]
    rank_ref[...] = jnp.concatenate(ranks, axis=0).astype(I32)
    carry_ref[...] = carry_ref[...] + jnp.sum(onehot, axis=1, keepdims=True)
    cnt_ref[...] = carry_ref[...].astype(I32)


def _outproj_router(x2, mix_a, mix_d, wo_bf, g_ffn, wr_t, br_col, tm):
    t, d = x2.shape
    ne = wr_t.shape[0]
    row = lambda w: pl.BlockSpec((tm, w), lambda i: (i, 0))
    full = lambda shp: pl.BlockSpec(shp, lambda i: (0, 0))
    small = lambda: pl.BlockSpec((TOP_K, tm), lambda i: (0, i))
    return pl.pallas_call(
        _outproj_router_kernel,
        grid=(t // tm,),
        in_specs=[row(d), row(WIDTH), row(WIDTH), full(wo_bf.shape), full((1, d)), full((ne, d)), full((ne, 1))],
        out_specs=[row(d), row(d), small(), small(), small(), full((ne, LANES))],
        out_shape=[
            jax.ShapeDtypeStruct((t, d), F32), jax.ShapeDtypeStruct((t, d), F32),
            jax.ShapeDtypeStruct((TOP_K, t), I32), jax.ShapeDtypeStruct((TOP_K, t), F32),
            jax.ShapeDtypeStruct((TOP_K, t), I32), jax.ShapeDtypeStruct((ne, LANES), I32),
        ],
        scratch_shapes=[pltpu.VMEM((ne, LANES), F32)],
        compiler_params=_cparams("arbitrary"),
        name="outproj_router",
    )(x2, mix_a, mix_d, wo_bf, g_ffn, wr_t, br_col)


def _dispatch_kernel(dest_ref, xn_ref, buf_in_ref, buf_ref, sem):
    del buf_in_ref
    i = pl.program_id(0)
    tm = xn_ref.shape[0]

    def row_copy(r, k):
        d = dest_ref[(i * tm + r) * TOP_K + k]
        return pltpu.make_async_copy(xn_ref.at[pl.ds(r, 1), :], buf_ref.at[pl.ds(d, 1), :], sem)

    def start(r, carry):
        for k in range(TOP_K):
            row_copy(r, k).start()
        return carry

    def wait(r, carry):
        for k in range(TOP_K):
            row_copy(r, k).wait()
        return carry

    lax.fori_loop(0, tm, start, 0)
    lax.fori_loop(0, tm, wait, 0)


def _dispatch(dest_flat, xn, buf, tm):
    t, d = xn.shape
    grid_spec = pltpu.PrefetchScalarGridSpec(
        num_scalar_prefetch=1,
        grid=(t // tm,),
        in_specs=[pl.BlockSpec((tm, d), lambda i, dest: (i, 0)), pl.BlockSpec(memory_space=pl.ANY)],
        out_specs=pl.BlockSpec(memory_space=pl.ANY),
        scratch_shapes=[pltpu.SemaphoreType.DMA(())],
    )
    return pl.pallas_call(
        _dispatch_kernel,
        grid_spec=grid_spec,
        out_shape=jax.ShapeDtypeStruct(buf.shape, buf.dtype),
        input_output_aliases={2: 0},
        compiler_params=_cparams("arbitrary"),
        name="moe_dispatch",
    )(dest_flat, xn, buf)


def _experts_kernel(be_ref, nu_ref, x_ref, wg_ref, bg_ref, wu_ref, bu_ref, wd_ref, bd_ref, y_ref,
                    wgb_ref, wub_ref, wdb_ref):
    i = pl.program_id(0)
    active = i < nu_ref[0]
    new_expert = (i == 0) | (be_ref[i] != be_ref[jnp.maximum(i - 1, 0)])

    @pl.when(active & new_expert)
    def _():
        wgb_ref[...] = wg_ref[0].astype(BF16)
        wub_ref[...] = wu_ref[0].astype(BF16)
        wdb_ref[...] = wd_ref[0].astype(BF16)

    @pl.when(active)
    def _():
        x = x_ref[...].astype(BF16)
        g = jnp.dot(x, wgb_ref[...], preferred_element_type=F32) + bg_ref[0]
        u = jnp.dot(x, wub_ref[...], preferred_element_type=F32) + bu_ref[0]
        g = jnp.minimum(g, SWIGLU_LIMIT)
        u = jnp.clip(u, -SWIGLU_LIMIT, SWIGLU_LIMIT)
        h = (g * jax.nn.sigmoid(SWIGLU_ALPHA * g) * (u + 1.0)).astype(BF16)
        y_ref[...] = jnp.dot(h, wdb_ref[...], preferred_element_type=F32) + bd_ref[0]

    @pl.when(i >= nu_ref[0])
    def _():
        y_ref[...] = jnp.zeros(y_ref.shape, F32)


def _experts(block_e, n_used, buf, wg, bg, wu, bu, wd, bd):
    n_rows, d = buf.shape
    dff = wg.shape[2]
    wspec = lambda shp: pl.BlockSpec((1,) + shp, lambda i, be, nu: (be[i], 0, 0))
    grid_spec = pltpu.PrefetchScalarGridSpec(
        num_scalar_prefetch=2,
        grid=(n_rows // MOE_ROWS,),
        in_specs=[pl.BlockSpec((MOE_ROWS, d), lambda i, be, nu: (i, 0)),
                  wspec((d, dff)), wspec((1, dff)), wspec((d, dff)), wspec((1, dff)),
                  wspec((dff, d)), wspec((1, d))],
        out_specs=pl.BlockSpec((MOE_ROWS, d), lambda i, be, nu: (i, 0)),
        scratch_shapes=[pltpu.VMEM((d, dff), BF16), pltpu.VMEM((d, dff), BF16), pltpu.VMEM((dff, d), BF16)],
    )
    return pl.pallas_call(
        _experts_kernel,
        grid_spec=grid_spec,
        out_shape=jax.ShapeDtypeStruct((n_rows, d), F32),
        compiler_params=_cparams("arbitrary"),
        name="moe_experts",
    )(block_e, n_used, buf, wg, bg, wu, bu, wd, bd)


def _combine_kernel(dest_ref, xres_ref, gate_ref, gfin_ref, yb_ref, out_ref, rows_ref, sem):
    i = pl.program_id(0)
    tm = xres_ref.shape[0]

    def row_copy(r, k):
        d = dest_ref[(i * tm + r) * TOP_K + k]
        return pltpu.make_async_copy(yb_ref.at[pl.ds(d, 1), :], rows_ref.at[k, pl.ds(r, 1), :], sem)

    def start(r, carry):
        for k in range(TOP_K):
            row_copy(r, k).start()
        return carry

    def wait(r, carry):
        for k in range(TOP_K):
            row_copy(r, k).wait()
        return carry

    lax.fori_loop(0, tm, start, 0)
    lax.fori_loop(0, tm, wait, 0)
    gates = gate_ref[...]
    y = gates[:, 0:1] * rows_ref[0]
    for k in range(1, TOP_K):
        y = y + gates[:, k:k + 1] * rows_ref[k]
    out_ref[...] = _norm_rows(xres_ref[...] + y, gfin_ref[...])


def _combine(dest_flat, xres, gates_col, g_final, yb, tm):
    t, d = xres.shape
    grid_spec = pltpu.PrefetchScalarGridSpec(
        num_scalar_prefetch=1,
        grid=(t // tm,),
        in_specs=[pl.BlockSpec((tm, d), lambda i, dest: (i, 0)),
                  pl.BlockSpec((tm, TOP_K), lambda i, dest: (i, 0)),
                  pl.BlockSpec((1, d), lambda i, dest: (0, 0)),
                  pl.BlockSpec(memory_space=pl.ANY)],
        out_specs=pl.BlockSpec((tm, d), lambda i, dest: (i, 0)),
        scratch_shapes=[pltpu.VMEM((TOP_K, tm, d), F32), pltpu.SemaphoreType.DMA(())],
    )
    return pl.pallas_call(
        _combine_kernel,
        grid_spec=grid_spec,
        out_shape=jax.ShapeDtypeStruct((t, d), F32),
        compiler_params=_cparams("arbitrary"),
        name="moe_combine",
    )(dest_flat, xres, gates_col, g_final, yb)


def kernel(x_prompt, x_sample, cache_k_moba, cache_v_moba, cache_k_diff, cache_v_diff, page_table,
           g_mix, w_in, lambda_q1, lambda_k1, lambda_q2, lambda_k2, g_subln, w_out, g_ffn,
           w_router, b_router, w_gate, b_gate, w_up, b_up, w_down, b_down, g_final):
    depth = w_in.shape[0]
    assert depth == 1, "single-layer step"
    batch, seq, d_model = x_prompt.shape
    dec_batch, dec_seq, _ = x_sample.shape
    assert dec_seq == 1
    n_pages = page_table.shape[1]
    n_experts = w_router.shape[2]
    layer = 0
    lam_init = 0.8 - 0.6 * math.exp(-0.3 * layer)
    ppb = MOBA_BLOCK // PAGE_SIZE
    assert n_pages % ppb == 0 and n_pages // ppb >= MOBA_TOPK

    w_in_bf = w_in[layer].astype(BF16)
    w_out_bf = w_out[layer].astype(BF16)
    g_mix_row = g_mix[layer][None]
    lams = (lambda_q1[layer][None], lambda_k1[layer][None], lambda_q2[layer][None], lambda_k2[layer][None])
    pt_flat = page_table.reshape(-1)

    tabs_p = _rope_tables(jnp.arange(seq, dtype=I32))
    qa, ka, va, qd, ka_hm, va_hm, kd_hm, vd_hm = _inproj_prompt(x_prompt, g_mix_row, w_in_bf, tabs_p, tm=256)
    mix_a_p = _moba_prompt(qa, ka, va)
    mix_d_p = _diff_prompt(qd, kd_hm, vd_hm, lams, g_subln[layer][:, None], lam_init)

    pos_s = n_pages * PAGE_SIZE + jnp.zeros((dec_batch,), I32)
    tabs_s = _rope_tables(pos_s)
    p_s = _inproj_sample(x_sample.reshape(1, dec_batch, d_model), g_mix_row, w_in_bf, tabs_s)[0]
    qa_s, ka_s, va_s, qd_s, kd_s, vd_s = (p_s[:, s * WIDTH:(s + 1) * WIDTH] for s in range(6))
    heads4 = lambda a, h: a.reshape(dec_batch, h, 1, WIDTH // h)
    cols4 = lambda a, h: a.reshape(dec_batch, h, WIDTH // h, 1)
    cache_kt_moba = jnp.swapaxes(cache_k_moba, 3, 4)
    cache_vt_moba = jnp.swapaxes(cache_v_moba, 3, 4)
    sel = _moba_gate(pt_flat, cols4(qa_s, N_HEADS_MOBA), cache_kt_moba, n_pages)
    sel = sel[:, :, :MOBA_TOPK]
    logical = sel[..., None] * ppb + jnp.arange(ppb, dtype=I32)
    phys = jnp.take_along_axis(page_table, logical.reshape(dec_batch, -1), axis=1)
    mix_a_s = _moba_sample_attn(phys.reshape(-1), heads4(qa_s, N_HEADS_MOBA), heads4(ka_s, N_HEADS_MOBA),
                                heads4(va_s, N_HEADS_MOBA), cache_kt_moba, cache_vt_moba, MOBA_TOPK * ppb,
                                layer)
    mix_d_s = _diff_rows(pt_flat, heads4(qd_s, N_HEADS_DIFF), heads4(kd_s, N_HEADS_DIFF),
                         heads4(vd_s, N_HEADS_DIFF), lams, g_subln[layer][None],
                         cache_k_diff, cache_v_diff, n_pages, lam_init, layer)

    wr_t = w_router[layer].T
    br_col = b_router[layer][:, None]
    g_ffn_row = g_ffn[layer][None]
    t_p = batch * seq
    xres_p, xn_p, eidx_p, gate_p, rank_p, cnt_p = _outproj_router(
        x_prompt.reshape(t_p, d_model), mix_a_p.reshape(t_p, WIDTH), mix_d_p.reshape(t_p, WIDTH),
        w_out_bf, g_ffn_row, wr_t, br_col, tm=512)
    xres_s, xn_s, eidx_s, gate_s, rank_s, cnt_s = _outproj_router(
        x_sample.reshape(dec_batch, d_model), mix_a_s.reshape(dec_batch, WIDTH),
        mix_d_s.reshape(dec_batch, WIDTH), w_out_bf, g_ffn_row, wr_t, br_col, tm=dec_batch)

    cnt_p = cnt_p[:, 0]
    sizes = cnt_p + cnt_s[:, 0]
    padded = (sizes + MOE_ROWS - 1) // MOE_ROWS * MOE_ROWS
    pends = jnp.cumsum(padded)
    pstart = pends - padded
    n_assign = (t_p + dec_batch) * TOP_K
    n_blocks = -(-(n_assign + n_experts * (MOE_ROWS - 1)) // MOE_ROWS)
    eids = jnp.arange(n_experts, dtype=I32)
    lookup = lambda table, eidx: jnp.sum(jnp.where(eidx[..., None] == eids, table, 0), axis=-1)
    dest_p = (lookup(pstart, eidx_p) + rank_p).T.reshape(-1)
    dest_s = (lookup(pstart + cnt_p, eidx_s) + rank_s).T.reshape(-1)
    block_starts = jnp.arange(n_blocks, dtype=I32) * MOE_ROWS
    block_e = jnp.minimum(jnp.sum((pends[None, :] <= block_starts[:, None]).astype(I32), axis=1),
                          n_experts - 1)
    n_used = (pends[-1:] // MOE_ROWS).astype(I32)

    buf = jnp.zeros((n_blocks * MOE_ROWS, d_model), F32)
    buf = _dispatch(dest_p, xn_p, buf, tm=512)
    buf = _dispatch(dest_s, xn_s, buf, tm=dec_batch)
    yb = _experts(block_e, n_used, buf,
                  w_gate[layer], b_gate[layer][:, None, :],
                  w_up[layer], b_up[layer][:, None, :],
                  w_down[layer], b_down[layer][:, None, :])
    g_fin_row = g_final[None]
    y_p = _combine(dest_p, xres_p, gate_p.T, g_fin_row, yb, tm=256)
    y_s = _combine(dest_s, xres_s, gate_s.T, g_fin_row, yb, tm=dec_batch)

    y_prompt = y_p.reshape(batch, seq, d_model)
    y_sample = y_s.reshape(dec_batch, 1, d_model)
    k_moba_sample = ka_s.reshape(dec_batch, 1, N_HEADS_MOBA, 1, HEAD_DIM)
    v_moba_sample = va_s.reshape(dec_batch, 1, N_HEADS_MOBA, 1, HEAD_DIM)
    k_diff_sample = kd_s.reshape(dec_batch, 1, N_HEADS_DIFF, 1, LANES)
    v_diff_sample = vd_s.reshape(dec_batch, 1, N_HEADS_DIFF, 1, LANES)
    return (y_prompt, y_sample, ka_hm, va_hm, kd_hm, vd_hm,
            k_moba_sample, v_moba_sample, k_diff_sample, v_diff_sample)
```
